```python
import jax, jax.numpy as jnp
from jax import lax
import numpy as np

D_MODEL = 2048
BATCH = 4
SEQ = 8192
DEPTH = 1
DEC_BATCH = 16
DEC_SEQ = 64
PAST_LEN = 4096

CHUNK = 64
Q_BLOCK = 128
RET_HEADS = 8
RET_DK = 64
RET_DV = 128
RET_WIDTH = RET_HEADS * RET_DV
MLA_HEADS = 8
MLA_NOPE = 128
MLA_ROPE = 64
MLA_DV = 128
MLA_WIDTH = MLA_HEADS * MLA_DV
Q_LORA = 512
KV_LORA = 512
MLA_SCALE = (MLA_NOPE + MLA_ROPE) ** -0.5
MIX_WIDTH = RET_WIDTH + MLA_WIDTH
IN_SIZES = (RET_HEADS * RET_DK, RET_HEADS * RET_DK, RET_WIDTH, RET_WIDTH, Q_LORA, KV_LORA, MLA_ROPE)
IN_WIDTH = sum(IN_SIZES)
IN_OFFSETS = tuple(int(o) for o in np.cumsum(IN_SIZES)[:-1])
D_FF = -(-8 * D_MODEL // (3 * 256)) * 256
ALPHA = (2 * DEPTH) ** 0.25
BETA = (8 * DEPTH) ** -0.25
ROPE_BASE = 10000.0
EPS = 1e-5

kernel_name = 'hymba_retention_mla_deepnorm_adaln_stream_step'


def _layernorm(x, g, b):
    xf = x.astype(jnp.float32)
    mu = jnp.mean(xf, axis=-1, keepdims=True)
    var = jnp.mean(jnp.square(xf - mu), axis=-1, keepdims=True)
    return ((xf - mu) * lax.rsqrt(var + EPS)).astype(x.dtype) * g + b


def _rmsnorm(x, g):
    xf = x.astype(jnp.float32)
    return (xf * lax.rsqrt(jnp.mean(xf * xf, axis=-1, keepdims=True) + EPS)).astype(x.dtype) * g


def _group_norm(o, g, b):
    of = o.astype(jnp.float32)
    mu = jnp.mean(of, axis=-1, keepdims=True)
    var = jnp.mean(jnp.square(of - mu), axis=-1, keepdims=True)
    y = ((of - mu) * lax.rsqrt(var + EPS)).astype(o.dtype)
    return y.reshape(o.shape[:2] + (-1,)) * g + b


def _rope(x, pos):
    half = x.shape[-1] // 2
    inv = ROPE_BASE ** (-jnp.arange(half, dtype=jnp.float32) / half)
    ang = pos.astype(jnp.float32)[:, None] * inv[None, :]
    cos = jnp.cos(ang)[None, :, None, :].astype(x.dtype)
    sin = jnp.sin(ang)[None, :, None, :].astype(x.dtype)
    x1, x2 = x[..., :half], x[..., half:]
    return jnp.concatenate([x1 * cos - x2 * sin, x2 * cos + x1 * sin], axis=-1)


def _ret_log_decay():
    h = jnp.arange(RET_HEADS, dtype=jnp.float32)
    return jnp.log1p(-jnp.exp2(-5.0 - h))


def _retention(q, k, v, s0):
    B, T, H, dk = q.shape
    dv = v.shape[-1]
    C = min(CHUNK, T)
    NC = T // C
    logg = _ret_log_decay()
    i = jnp.arange(C, dtype=jnp.float32)
    diff = i[:, None] - i[None, :]
    dmask = jnp.where(diff[None] >= 0.0,
                      jnp.exp(jnp.maximum(diff, 0.0)[None] * logg[:, None, None]), 0.0).astype(q.dtype)
    q_dec = jnp.exp((i + 1.0)[:, None] * logg[None, :]).astype(q.dtype)
    k_dec = jnp.exp((C - 1.0 - i)[:, None] * logg[None, :]).astype(q.dtype)
    c_dec = jnp.exp(C * logg).astype(q.dtype)
    qc = q.reshape(B, NC, C, H, dk)
    kc = k.reshape(B, NC, C, H, dk)
    vc = v.reshape(B, NC, C, H, dv)
    scores = jnp.einsum('bnihd,bnjhd->bnhij', qc, kc) * dmask
    intra = jnp.einsum('bnhij,bnjhe->bnihe', scores, vc)
    kv = jnp.einsum('bnjhd,bnjhe->bnhde', kc * k_dec[:, :, None], vc)

    def step(s, kv_n):
        return s * c_dec[:, None, None] + kv_n, s

    s_last, s_in = lax.scan(step, s0.astype(kv.dtype), jnp.moveaxis(kv, 1, 0))
    s_in = jnp.moveaxis(s_in, 0, 1)
    inter = jnp.einsum('bnihd,bnhde->bnihe', qc * q_dec[:, :, None], s_in)
    return (intra + inter).reshape(B, T, H, dv), s_last


def _attend(qn, qr, q_chunk, kn, kr, v, k_chunk):
    s = jnp.einsum('bqhd,bkhd->bhqk', qn, kn) + jnp.einsum('bqhd,bkd->bhqk', qr, kr)
    s = s.astype(jnp.float32) * MLA_SCALE
    allowed = (k_chunk[None, :] <= q_chunk[:, None])[None, None]
    s = jnp.where(allowed, s, jnp.finfo(jnp.float32).min)
    p = jax.nn.softmax(s, axis=-1).astype(v.dtype)
    return jnp.einsum('bhqk,bkhd->bqhd', p, v)


def _mixer(h, pos, ckv_past, kr_past, ret_s0, w_in, g_cq, g_ckv, w_uq, w_uk, w_uv, g_ret, b_ret, w_out):
    B, T, _ = h.shape
    rq, rk, rv, rg, cq, ckv, kr = jnp.split(h @ w_in, IN_OFFSETS, axis=-1)
    rq = _rope(rq.reshape(B, T, RET_HEADS, RET_DK), pos) * (RET_DK ** -0.5)
    rk = _rope(rk.reshape(B, T, RET_HEADS, RET_DK), pos)
    rv = rv.reshape(B, T, RET_HEADS, RET_DV)
    o_ret, ret_s1 = _retention(rq, rk, rv, ret_s0)
    o_ret = jax.nn.silu(rg) * _group_norm(o_ret, g_ret, b_ret)
    q = jnp.einsum('btr,rhd->bthd', _rmsnorm(cq, g_cq), w_uq)
    qn = q[..., :MLA_NOPE]
    qr = _rope(q[..., MLA_NOPE:], pos)
    ckv = _rmsnorm(ckv, g_ckv)
    kr = _rope(kr[:, :, None, :], pos)[:, :, 0, :]
    q_chunk = pos // CHUNK
    if ckv_past is None:
        ckv_all, kr_all, k_chunk = ckv, kr, q_chunk
    else:
        past = ckv_past.shape[1]
        ckv_all = jnp.concatenate([ckv_past, ckv], axis=1)
        kr_all = jnp.concatenate([kr_past, kr], axis=1)
        k_chunk = jnp.concatenate([jnp.arange(past, dtype=pos.dtype), pos]) // CHUNK
    kn = jnp.einsum('bsr,rhd->bshd', ckv_all, w_uk)
    v = jnp.einsum('bsr,rhd->bshd', ckv_all, w_uv)
    if T > Q_BLOCK and T % Q_BLOCK == 0:
        nb = T // Q_BLOCK
        blocks = lambda a: jnp.moveaxis(a.reshape((B, nb, Q_BLOCK) + a.shape[2:]), 1, 0)
        o = lax.map(lambda a: _attend(a[0], a[1], a[2], kn, kr_all, v, k_chunk),
                    (blocks(qn), blocks(qr), q_chunk.reshape(nb, Q_BLOCK)))
        o_mla = jnp.moveaxis(o, 0, 1).reshape(B, T, MLA_WIDTH)
    else:
        o_mla = _attend(qn, qr, q_chunk, kn, kr_all, v, k_chunk).reshape(B, T, MLA_WIDTH)
    y = jnp.concatenate([o_ret, o_mla], axis=-1) @ w_out
    return y, ckv, kr, ret_s1


def _layer(x, c, pos, ckv_past, kr_past, ret_s0, w_ada, b_ada, w_in, g_cq, g_ckv, w_uq, w_uk, w_uv,
           g_ret, b_ret, w_out, ln1_g, ln1_b, w_gate, w_up, w_down, ln2_g, ln2_b):
    mod = (jax.nn.silu(c) @ w_ada + b_ada)[:, None, :]
    sh1, sc1, g1, sh2, sc2, g2 = jnp.split(mod, 6, axis=-1)
    h = x * (1.0 + sc1) + sh1
    m, ckv, kr, ret_s1 = _mixer(h, pos, ckv_past, kr_past, ret_s0, w_in, g_cq, g_ckv, w_uq, w_uk, w_uv,
                                g_ret, b_ret, w_out)
    x = _layernorm(ALPHA * x + g1 * m, ln1_g, ln1_b)
    h = x * (1.0 + sc2) + sh2
    f = (jax.nn.silu(h @ w_gate) * (h @ w_up)) @ w_down
    x = _layernorm(ALPHA * x + g2 * f, ln2_g, ln2_b)
    return x, ckv, kr, ret_s1


def setup_inputs(seed: int = 0) -> dict:
    key = jax.random.key(seed)
    ks = jax.random.split(key, 32)
    f32 = jnp.float32
    L = DEPTH

    def nrm(k, shape, scale):
        return jax.random.normal(k, shape, f32) * scale

    def gain(k, shape):
        return 1.0 + 0.01 * jax.random.normal(k, shape, f32)

    n_qk = 2 * RET_HEADS * RET_DK
    col_scale = jnp.concatenate([jnp.ones((n_qk,), f32), jnp.full((RET_WIDTH,), BETA, f32),
                                 jnp.ones((IN_WIDTH - n_qk - RET_WIDTH,), f32)])
    return {
        'x_prompt': nrm(ks[0], (BATCH, SEQ, D_MODEL), 1.0),
        'x_sample': nrm(ks[1], (DEC_BATCH, DEC_SEQ, D_MODEL), 1.0),
        'c_prompt': nrm(ks[2], (BATCH, D_MODEL), 1.0),
        'c_sample': nrm(ks[3], (DEC_BATCH, D_MODEL), 1.0),
        'cache_mla_ckv': nrm(ks[4], (L, DEC_BATCH, PAST_LEN, KV_LORA), 1.0),
        'cache_mla_krope': nrm(ks[5], (L, DEC_BATCH, PAST_LEN, MLA_ROPE), 1.0),
        'state_ret': nrm(ks[6], (L, DEC_BATCH, RET_HEADS, RET_DK, RET_DV), 1.0),
        'w_ada': nrm(ks[7], (L, D_MODEL, 6 * D_MODEL), 0.5 * D_MODEL ** -0.5),
        'b_ada': nrm(ks[8], (L, 6 * D_MODEL), 0.01),
        'w_in': nrm(ks[9], (L, D_MODEL, IN_WIDTH), D_MODEL ** -0.5) * col_scale,
        'g_cq': gain(ks[10], (L, Q_LORA)),
        'g_ckv': gain(ks[11], (L, KV_LORA)),
        'w_uq': nrm(ks[12], (L, Q_LORA, MLA_HEADS, MLA_NOPE + MLA_ROPE), Q_LORA ** -0.5),
        'w_uk': nrm(ks[13], (L, KV_LORA, MLA_HEADS, MLA_NOPE), KV_LORA ** -0.5),
        'w_uv': nrm(ks[14], (L, KV_LORA, MLA_HEADS, MLA_DV), BETA * KV_LORA ** -0.5),
        'g_ret': gain(ks[15], (L, RET_WIDTH)),
        'b_ret': nrm(ks[16], (L, RET_WIDTH), 0.01),
        'w_out': nrm(ks[17], (L, MIX_WIDTH, D_MODEL), BETA * MIX_WIDTH ** -0.5),
        'ln1_g': gain(ks[18], (L, D_MODEL)),
        'ln1_b': nrm(ks[19], (L, D_MODEL), 0.01),
        'w_gate': nrm(ks[20], (L, D_MODEL, D_FF), BETA * D_MODEL ** -0.5),
        'w_up': nrm(ks[21], (L, D_MODEL, D_FF), BETA * D_MODEL ** -0.5),
        'w_down': nrm(ks[22], (L, D_FF, D_MODEL), BETA * D_FF ** -0.5),
        'ln2_g': gain(ks[23], (L, D_MODEL)),
        'ln2_b': nrm(ks[24], (L, D_MODEL), 0.01),
    }


def reference(x_prompt, x_sample, c_prompt, c_sample, cache_mla_ckv, cache_mla_krope, state_ret,
              w_ada, b_ada, w_in, g_cq, g_ckv, w_uq, w_uk, w_uv, g_ret, b_ret, w_out,
              ln1_g, ln1_b, w_gate, w_up, w_down, ln2_g, ln2_b):
    t_p = x_prompt.shape[1]
    t_s = x_sample.shape[1]
    past = cache_mla_ckv.shape[2]
    pos_p = jnp.arange(t_p, dtype=jnp.int32)
    pos_s = past + jnp.arange(t_s, dtype=jnp.int32)
    yp, ys = x_prompt, x_sample
    ckv_p_l, kr_p_l, rs_p_l, ckv_s_l, kr_s_l, rs_s_l = [], [], [], [], [], []
    for l in range(DEPTH):
        wts = (w_ada[l], b_ada[l], w_in[l], g_cq[l], g_ckv[l], w_uq[l], w_uk[l], w_uv[l],
               g_ret[l], b_ret[l], w_out[l], ln1_g[l], ln1_b[l], w_gate[l], w_up[l], w_down[l],
               ln2_g[l], ln2_b[l])
        s0 = jnp.zeros((yp.shape[0], RET_HEADS, RET_DK, RET_DV), yp.dtype)
        yp, ckv_p, kr_p, rs_p = _layer(yp, c_prompt, pos_p, None, None, s0, *wts)
        ys, ckv_s, kr_s, rs_s = _layer(ys, c_sample, pos_s, cache_mla_ckv[l], cache_mla_krope[l],
                                       state_ret[l], *wts)
        ckv_p_l.append(ckv_p)
        kr_p_l.append(kr_p)
        rs_p_l.append(rs_p)
        ckv_s_l.append(ckv_s)
        kr_s_l.append(kr_s)
        rs_s_l.append(rs_s)
    new_ckv_prompt = jnp.stack(ckv_p_l, axis=0)
    new_krope_prompt = jnp.stack(kr_p_l, axis=0)
    ret_state_prompt = jnp.stack(rs_p_l, axis=0)
    new_ckv_sample = jnp.stack(ckv_s_l, axis=0)
    new_krope_sample = jnp.stack(kr_s_l, axis=0)
    ret_state_sample = jnp.stack(rs_s_l, axis=0)
    return (yp, ys, new_ckv_prompt, new_krope_prompt, ret_state_prompt, new_ckv_sample, new_krope_sample, ret_state_sample)
```

```python
import functools
import math

import jax
import jax.numpy as jnp
import numpy as np
from jax import lax
from jax.experimental import pallas as pl
from jax.experimental.pallas import tpu as pltpu

F32 = jnp.float32
BF16 = jnp.bfloat16

D_MODEL = 2048
CHUNK = 64
RET_HEADS = 8
RET_DK = 64
RET_DV = 128
RET_QK = RET_HEADS * RET_DK
RET_WIDTH = RET_HEADS * RET_DV
MLA_HEADS = 8
MLA_NOPE = 128
MLA_ROPE = 64
MLA_DV = 128
MLA_QK = MLA_NOPE + MLA_ROPE
MLA_WIDTH = MLA_HEADS * MLA_DV
Q_LORA = 512
KV_LORA = 512
MLA_SCALE = MLA_QK ** -0.5
D_FF = 5632
DEPTH = 1
ALPHA = (2 * DEPTH) ** 0.25
ROPE_BASE = 10000.0
EPS = 1e-5
LOG2E = math.log2(math.e)
NEG_BIG = -1e30

LANES = 128
ROW_TILE = 512
FFN_ROW_TILE = 1024
FFN_COL_TILE = 512
RET_CHUNK = 256
ATTN_TILE = 512
ADA_COL_TILE = 1024
VMEM_LIMIT = 56 * 1024 * 1024


def _params(sem, vmem=VMEM_LIMIT):
    return pltpu.CompilerParams(dimension_semantics=sem, vmem_limit_bytes=vmem)


def _const_spec(shape):
    zeros = (0,) * len(shape)
    return pl.BlockSpec(shape, lambda *_: zeros, pipeline_mode=pl.Buffered(1))


def _dot(a, b):
    return jnp.dot(a, b, preferred_element_type=F32)


def _dot_nt(a, b):
    return lax.dot_general(a, b, (((1,), (1,)), ((), ())), preferred_element_type=F32)


def _dot_tn(a, b):
    return lax.dot_general(a, b, (((0,), (0,)), ((), ())), preferred_element_type=F32)


def _silu(x):
    return x * jax.nn.sigmoid(x)


def _row_tiling(batch, seq, rows):
    if seq >= rows:
        assert seq % rows == 0
        return 1, rows
    nb = min(batch, rows // seq)
    assert batch % nb == 0
    return nb, seq


def _rope_cols(y, cos, sin, first_half):
    outs = []
    for g in range(y.shape[1] // LANES):
        yg = y[:, g * LANES:(g + 1) * LANES]
        rot = jnp.where(first_half, pltpu.roll(yg, LANES - 32, 1), pltpu.roll(yg, 32, 1))
        outs.append(yg * cos + rot * sin)
    return outs[0] if len(outs) == 1 else jnp.concatenate(outs, axis=1)


def _rope_tables(pos):
    half = 32
    inv = ROPE_BASE ** (-jnp.arange(half, dtype=F32) / half)
    ang = pos.astype(F32)[:, None] * inv[None, :]
    cos = jnp.cos(ang)
    sin = jnp.sin(ang)
    cos128 = jnp.concatenate([cos, cos, cos, cos], axis=1)
    sin128 = jnp.concatenate([-sin, sin, -sin, sin], axis=1)
    return cos128, sin128


def _tile_rows(t, nb):
    if nb == 1:
        return t
    return jnp.broadcast_to(t[None], (nb,) + t.shape).reshape(nb * t.shape[0], t.shape[1])


def _ada_kernel(c_ref, w_ref, b_ref, o_ref):
    c = c_ref[...]
    a = _silu(c).astype(BF16)
    o_ref[...] = _dot(a, w_ref[...].astype(BF16)) + b_ref[...]


def _ada(c, w_ada, b_ada):
    nrow = c.shape[0]
    ncol = w_ada.shape[1]
    tn = ADA_COL_TILE
    return pl.pallas_call(
        _ada_kernel,
        grid=(ncol // tn,),
        in_specs=[_const_spec((nrow, D_MODEL)),
                  pl.BlockSpec((D_MODEL, tn), lambda j: (0, j)),
                  pl.BlockSpec((1, tn), lambda j: (0, j))],
        out_specs=pl.BlockSpec((nrow, tn), lambda j: (0, j)),
        out_shape=jax.ShapeDtypeStruct((nrow, ncol), F32),
        compiler_params=_params(("arbitrary",)),
        name="ada_mod",
    )(c, w_ada, b_ada.reshape(1, ncol))


IN_W = 2 * RET_QK + 2 * RET_WIDTH + Q_LORA + KV_LORA + 2 * MLA_ROPE


def _inproj_kernel(x_ref, sc_ref, sh_ref, w_ref, gcq_ref, gckv_ref, cos_ref, sin_ref,
                   rq_ref, rk_ref, rv_ref, rg_ref, cq_ref, ckv_ref, kr_ref, *, nb, rb):
    m = nb * rb
    h = x_ref[...] * (1.0 + sc_ref[...]) + sh_ref[...]
    hb = h.astype(BF16).reshape(m, D_MODEL)
    cos = _tile_rows(cos_ref[...], nb)
    sin = _tile_rows(sin_ref[...], nb)
    lane = lax.broadcasted_iota(jnp.int32, (m, LANES), 1)
    first_half = (lane % 64) < 32

    def proj(lo, hi):
        return _dot(hb, w_ref[:, lo:hi])

    def rmsnorm(y, g):
        ms = jnp.mean(y * y, axis=-1, keepdims=True)
        return y * lax.rsqrt(ms + EPS) * g

    o = 0
    rq = _rope_cols(proj(o, o + RET_QK), cos, sin, first_half) * (RET_DK ** -0.5)
    rq_ref[...] = rq.astype(BF16).reshape(nb, rb, RET_QK)
    o += RET_QK
    rk = _rope_cols(proj(o, o + RET_QK), cos, sin, first_half)
    rk_ref[...] = rk.astype(BF16).reshape(nb, rb, RET_QK)
    o += RET_QK
    rv_ref[...] = proj(o, o + RET_WIDTH).astype(BF16).reshape(nb, rb, RET_WIDTH)
    o += RET_WIDTH
    rg_ref[...] = proj(o, o + RET_WIDTH).astype(BF16).reshape(nb, rb, RET_WIDTH)
    o += RET_WIDTH
    cq_ref[...] = rmsnorm(proj(o, o + Q_LORA), gcq_ref[...]).astype(BF16).reshape(nb, rb, Q_LORA)
    o += Q_LORA
    ckv_ref[...] = rmsnorm(proj(o, o + KV_LORA), gckv_ref[...]).reshape(nb, rb, KV_LORA)
    o += KV_LORA
    kr = _rope_cols(proj(o, o + LANES), cos, sin, first_half)
    kr_ref[...] = kr[:, :MLA_ROPE].reshape(nb, rb, MLA_ROPE)


def _inproj(x, sc, sh, w_in_b, g_cq, g_ckv, cos, sin):
    batch, seq, _ = x.shape
    nb, rb = _row_tiling(batch, seq, ROW_TILE)
    grid = (batch // nb, seq // rb)
    row = lambda w: pl.BlockSpec((nb, rb, w), lambda b, i: (b, i, 0))
    mod = pl.BlockSpec((nb, 1, D_MODEL), lambda b, i: (b, 0, 0))
    tab = pl.BlockSpec((rb, LANES), lambda b, i: (i, 0))
    sds = lambda w, dt: jax.ShapeDtypeStruct((batch, seq, w), dt)
    return pl.pallas_call(
        functools.partial(_inproj_kernel, nb=nb, rb=rb),
        grid=grid,
        in_specs=[row(D_MODEL), mod, mod, _const_spec((D_MODEL, IN_W)),
                  _const_spec((1, Q_LORA)), _const_spec((1, KV_LORA)), tab, tab],
        out_specs=[row(RET_QK), row(RET_QK), row(RET_WIDTH), row(RET_WIDTH),
                   row(Q_LORA), row(KV_LORA), row(MLA_ROPE)],
        out_shape=[sds(RET_QK, BF16), sds(RET_QK, BF16), sds(RET_WIDTH, BF16), sds(RET_WIDTH, BF16),
                   sds(Q_LORA, BF16), sds(KV_LORA, F32), sds(MLA_ROPE, F32)],
        compiler_params=_params(("parallel", "parallel")),
        name="in_proj",
    )(x, sc, sh, w_in_b, g_cq, g_ckv, cos, sin)


def _ret_tables(c):
    hh = jnp.arange(RET_HEADS, dtype=F32)
    logg = jnp.log1p(-jnp.exp2(-5.0 - hh))
    i = jnp.arange(c, dtype=F32)
    diff = i[:, None] - i[None, :]
    dmask = jnp.where(diff[None] >= 0.0, jnp.exp(jnp.maximum(diff, 0.0)[None] * logg[:, None, None]), 0.0)
    qdec = jnp.exp((i + 1.0)[:, None] * logg[None, :])
    kdec = jnp.exp((c - 1.0 - i)[:, None] * logg[None, :])
    cdec = jnp.exp(c * logg)
    qdec = jnp.repeat(qdec, RET_DK, axis=1)
    kdec = jnp.repeat(kdec, RET_DK, axis=1)
    r = jnp.arange(2 * RET_DK)[:, None] // RET_DK
    col = jnp.arange(2 * RET_DV)[None, :] // RET_DV
    pair = jnp.arange(RET_HEADS // 2)[:, None, None]
    sdec = jnp.where(r[None] == col[None], cdec[2 * pair + r[None]], 0.0)
    return dmask.astype(F32), qdec.astype(F32), kdec.astype(F32), sdec.astype(F32)


def _ret_kernel(rq_ref, rk_ref, rv_ref, rg_ref, s0_ref, dmask_ref, qdec_ref, kdec_ref, sdec_ref,
                gret_ref, bret_ref, o_ref, so_ref, s_scr, *, c):
    n = pl.program_id(1)
    npair = RET_HEADS // 2

    @pl.when(n == 0)
    def _():
        s_scr[...] = jnp.zeros(s_scr.shape, F32)
        for p in range(npair):
            s_scr[p, 0:RET_DK, 0:RET_DV] = s0_ref[0, 2 * p]
            s_scr[p, RET_DK:2 * RET_DK, RET_DV:2 * RET_DV] = s0_ref[0, 2 * p + 1]

    q = rq_ref[0]
    k = rk_ref[0]
    qd = (q.astype(F32) * qdec_ref[...]).astype(BF16)
    kd = (k.astype(F32) * kdec_ref[...]).astype(BF16)
    lane = lax.broadcasted_iota(jnp.int32, (c, LANES), 1)
    lo = lane < RET_DK
    zero = jnp.zeros((c, LANES), BF16)
    for p in range(npair):
        qp = q[:, p * LANES:(p + 1) * LANES]
        kp = k[:, p * LANES:(p + 1) * LANES]
        vp = rv_ref[0, :, p * 2 * RET_DV:(p + 1) * 2 * RET_DV]
        sc0 = _dot_nt(jnp.where(lo, qp, zero), kp) * dmask_ref[2 * p]
        sc1 = _dot_nt(jnp.where(lo, zero, qp), kp) * dmask_ref[2 * p + 1]
        intra0 = _dot(sc0.astype(BF16), vp[:, :RET_DV])
        intra1 = _dot(sc1.astype(BF16), vp[:, RET_DV:])
        s = s_scr[p]
        inter = _dot(qd[:, p * LANES:(p + 1) * LANES], s.astype(BF16))
        kv = _dot_tn(kd[:, p * LANES:(p + 1) * LANES], vp)
        sdec = sdec_ref[p]
        s_scr[p] = s * sdec + jnp.where(sdec > 0.0, kv, 0.0)
        for hh in range(2):
            hd = 2 * p + hh
            cols = slice(hd * RET_DV, (hd + 1) * RET_DV)
            oh = (intra0 if hh == 0 else intra1) + inter[:, hh * RET_DV:(hh + 1) * RET_DV]
            mu = jnp.mean(oh, axis=-1, keepdims=True)
            d = oh - mu
            var = jnp.mean(d * d, axis=-1, keepdims=True)
            y = d * lax.rsqrt(var + EPS) * gret_ref[:, cols] + bret_ref[:, cols]
            gate = rg_ref[0, :, cols].astype(F32)
            o_ref[0, :, cols] = (_silu(gate) * y).astype(BF16)

    @pl.when(n == pl.num_programs(1) - 1)
    def _():
        for p in range(npair):
            so_ref[0, 2 * p] = s_scr[p, 0:RET_DK, 0:RET_DV]
            so_ref[0, 2 * p + 1] = s_scr[p, RET_DK:2 * RET_DK, RET_DV:2 * RET_DV]


def _retention(rq, rk, rv, rg, s0, g_ret, b_ret):
    batch, seq, _ = rq.shape
    c = min(RET_CHUNK, seq)
    assert seq % c == 0
    dmask, qdec, kdec, sdec = _ret_tables(c)
    row = lambda w: pl.BlockSpec((1, c, w), lambda b, n: (b, n, 0))
    st = pl.BlockSpec((1, RET_HEADS, RET_DK, RET_DV), lambda b, n: (b, 0, 0, 0))
    return pl.pallas_call(
        functools.partial(_ret_kernel, c=c),
        grid=(batch, seq // c),
        in_specs=[row(RET_QK), row(RET_QK), row(RET_WIDTH), row(RET_WIDTH), st,
                  _const_spec(dmask.shape), _const_spec(qdec.shape), _const_spec(kdec.shape),
                  _const_spec(sdec.shape), _const_spec((1, RET_WIDTH)), _const_spec((1, RET_WIDTH))],
        out_specs=[row(RET_WIDTH), st],
        out_shape=[jax.ShapeDtypeStruct((batch, seq, RET_WIDTH), BF16),
                   jax.ShapeDtypeStruct((batch, RET_HEADS, RET_DK, RET_DV), F32)],
        scratch_shapes=[pltpu.VMEM((RET_HEADS // 2, 2 * RET_DK, 2 * RET_DV), F32)],
        compiler_params=_params(("parallel", "arbitrary")),
        name="retention",
    )(rq, rk, rv, rg, s0, dmask, qdec, kdec, sdec, g_ret, b_ret)


def _upproj_kernel(cq_ref, ckv_ref, kr_ref, cos_ref, sin_ref, wqn_ref, wqr_ref, wk_ref, wv_ref,
                   q_ref, k_ref, v_ref, *, rb):
    cq = cq_ref[0]
    ckv = ckv_ref[0].astype(BF16)
    krb = kr_ref[0].astype(BF16)
    lane = lax.broadcasted_iota(jnp.int32, (rb, LANES), 1)
    first_half = (lane % 64) < 32
    qscale = MLA_SCALE * LOG2E
    qn = (_dot(cq, wqn_ref[...]) * qscale).astype(BF16)
    qr = (_rope_cols(_dot(cq, wqr_ref[...]), cos_ref[...], sin_ref[...], first_half) * qscale).astype(BF16)
    kn = _dot(ckv, wk_ref[...]).astype(BF16)
    v = _dot(ckv, wv_ref[...]).astype(BF16)
    for hd in range(MLA_HEADS):
        q_ref[0, hd, :, 0:MLA_NOPE] = qn[:, hd * MLA_NOPE:(hd + 1) * MLA_NOPE]
        q_ref[0, hd, :, MLA_NOPE:MLA_QK] = qr[:, hd * MLA_ROPE:(hd + 1) * MLA_ROPE]
        k_ref[0, hd, :, 0:MLA_NOPE] = kn[:, hd * MLA_NOPE:(hd + 1) * MLA_NOPE]
        k_ref[0, hd, :, MLA_NOPE:MLA_QK] = krb
        v_ref[0, hd] = v[:, hd * MLA_DV:(hd + 1) * MLA_DV]


def _upproj(cq, ckv, kr, cos, sin, wqn, wqr, wk, wv):
    batch, seq, _ = cq.shape
    rb = min(ROW_TILE, seq)
    assert seq % rb == 0
    row = lambda w: pl.BlockSpec((1, rb, w), lambda b, i: (b, i, 0))
    tab = pl.BlockSpec((rb, LANES), lambda b, i: (i, 0))
    head = lambda w: pl.BlockSpec((1, MLA_HEADS, rb, w), lambda b, i: (b, 0, i, 0))
    sds = lambda w: jax.ShapeDtypeStruct((batch, MLA_HEADS, seq, w), BF16)
    return pl.pallas_call(
        functools.partial(_upproj_kernel, rb=rb),
        grid=(batch, seq // rb),
        in_specs=[row(Q_LORA), row(KV_LORA), row(MLA_ROPE), tab, tab,
                  _const_spec(wqn.shape), _const_spec(wqr.shape), _const_spec(wk.shape), _const_spec(wv.shape)],
        out_specs=[head(MLA_QK), head(MLA_QK), head(MLA_DV)],
        out_shape=[sds(MLA_QK), sds(MLA_QK), sds(MLA_DV)],
        compiler_params=_params(("parallel", "parallel")),
        name="mla_up_proj",
    )(cq, ckv, kr, cos, sin, wqn, wqr, wk, wv)


def _flash_kernel(q_ref, k_ref, v_ref, o_ref, m_scr, l_scr, acc_scr, *, t):
    i = pl.program_id(2)
    q = q_ref[0, 0]
    m_scr[...] = jnp.full(m_scr.shape, -jnp.inf, F32)
    l_scr[...] = jnp.zeros(l_scr.shape, F32)
    acc_scr[...] = jnp.zeros(acc_scr.shape, F32)

    def block(j, masked):
        start = pl.multiple_of(j * t, t)
        kb = k_ref[0, 0, pl.ds(start, t), :]
        vb = v_ref[0, 0, pl.ds(start, t), :]
        s = _dot_nt(q, kb)
        if masked:
            qc = lax.broadcasted_iota(jnp.int32, (t, t), 0) // CHUNK
            kc = lax.broadcasted_iota(jnp.int32, (t, t), 1) // CHUNK
            s = jnp.where(kc <= qc, s, NEG_BIG)
        m_prev = m_scr[...]
        m_new = jnp.maximum(m_prev, jnp.max(s, axis=1, keepdims=True))
        alpha = jnp.exp2(m_prev - m_new)
        p = jnp.exp2(s - m_new)
        l_scr[...] = alpha * l_scr[...] + jnp.sum(p, axis=1, keepdims=True)
        acc_scr[...] = alpha * acc_scr[...] + _dot(p.astype(BF16), vb)
        m_scr[...] = m_new

    def body(j, carry):
        block(j, False)
        return carry

    lax.fori_loop(0, i, body, 0)
    block(i, True)
    o_ref[0] = (acc_scr[...] / l_scr[...]).astype(BF16)


def _flash(q, k, v):
    batch, heads, seq, _ = q.shape
    t = min(ATTN_TILE, seq)
    assert seq % t == 0 and t % CHUNK == 0
    return pl.pallas_call(
        functools.partial(_flash_kernel, t=t),
        grid=(batch, heads, seq // t),
        in_specs=[pl.BlockSpec((1, 1, t, MLA_QK), lambda b, h, i: (b, h, i, 0)),
                  pl.BlockSpec((1, 1, seq, MLA_QK), lambda b, h, i: (b, h, 0, 0)),
                  pl.BlockSpec((1, 1, seq, MLA_DV), lambda b, h, i: (b, h, 0, 0))],
        out_specs=pl.BlockSpec((1, t, MLA_DV), lambda b, h, i: (b, i, h)),
        out_shape=jax.ShapeDtypeStruct((batch, seq, MLA_WIDTH), BF16),
        scratch_shapes=[pltpu.VMEM((t, 1), F32), pltpu.VMEM((t, 1), F32), pltpu.VMEM((t, MLA_DV), F32)],
        compiler_params=_params(("parallel", "parallel", "arbitrary")),
        name="mla_flash",
    )(q, k, v)


def _cached_attn_kernel(cq_ref, ckvn_ref, krn_ref, ckvp_ref, krp_ref, cos_ref, sin_ref,
                        wqn_ref, wqr_ref, wkt_ref, wv_ref, o_ref, *, tq, past):
    cq = cq_ref[0]
    lane = lax.broadcasted_iota(jnp.int32, (tq, LANES), 1)
    first_half = (lane % 64) < 32
    qscale = MLA_SCALE * LOG2E
    qn = (_dot(cq, wqn_ref[...]) * qscale).astype(BF16)
    qr = (_rope_cols(_dot(cq, wqr_ref[...]), cos_ref[...], sin_ref[...], first_half) * qscale).astype(BF16)
    qlat = jnp.concatenate(
        [_dot(qn[:, hd * MLA_NOPE:(hd + 1) * MLA_NOPE], wkt_ref[hd]) for hd in range(MLA_HEADS)],
        axis=0).astype(BF16)
    qrr = jnp.concatenate([qr[:, hd * MLA_ROPE:(hd + 1) * MLA_ROPE] for hd in range(MLA_HEADS)], axis=0)
    ckv_p = ckvp_ref[0].astype(BF16)
    kr_p = krp_ref[0].astype(BF16)
    ckv_n = ckvn_ref[0].astype(BF16)
    kr_n = krn_ref[0].astype(BF16)
    s_p = _dot_nt(qlat, ckv_p) + _dot_nt(qrr, kr_p)
    s_n = _dot_nt(qlat, ckv_n) + _dot_nt(qrr, kr_n)
    rows = MLA_HEADS * tq
    qpos = past + lax.broadcasted_iota(jnp.int32, (rows, tq), 0) % tq
    kpos = past + lax.broadcasted_iota(jnp.int32, (rows, tq), 1)
    s_n = jnp.where(kpos // CHUNK <= qpos // CHUNK, s_n, NEG_BIG)
    m = jnp.maximum(jnp.max(s_p, axis=1, keepdims=True), jnp.max(s_n, axis=1, keepdims=True))
    p_p = jnp.exp2(s_p - m)
    p_n = jnp.exp2(s_n - m)
    l = jnp.sum(p_p, axis=1, keepdims=True) + jnp.sum(p_n, axis=1, keepdims=True)
    olat = (_dot(p_p.astype(BF16), ckv_p) + _dot(p_n.astype(BF16), ckv_n)) / l
    olat = olat.astype(BF16)
    for hd in range(MLA_HEADS):
        o_ref[0, :, hd * MLA_DV:(hd + 1) * MLA_DV] = _dot(olat[hd * tq:(hd + 1) * tq], wv_ref[hd]).astype(BF16)


def _cached_attn(cq, ckv_new, kr_new, ckv_past, kr_past, cos, sin, wqn, wqr, wkt, wvh):
    batch, tq, _ = cq.shape
    past = ckv_past.shape[1]
    assert past % CHUNK == 0
    one = lambda n, w: pl.BlockSpec((1, n, w), lambda b: (b, 0, 0))
    return pl.pallas_call(
        functools.partial(_cached_attn_kernel, tq=tq, past=past),
        grid=(batch,),
        in_specs=[one(tq, Q_LORA), one(tq, KV_LORA), one(tq, MLA_ROPE), one(past, KV_LORA), one(past, MLA_ROPE),
                  _const_spec(cos.shape), _const_spec(sin.shape), _const_spec(wqn.shape), _const_spec(wqr.shape),
                  _const_spec(wkt.shape), _const_spec(wvh.shape)],
        out_specs=one(tq, MLA_WIDTH),
        out_shape=jax.ShapeDtypeStruct((batch, tq, MLA_WIDTH), BF16),
        compiler_params=_params(("parallel",)),
        name="mla_cached",
    )(cq, ckv_new, kr_new, ckv_past, kr_past, cos, sin, wqn, wqr, wkt, wvh)


def _layernorm(y, g, b):
    mu = jnp.mean(y, axis=-1, keepdims=True)
    d = y - mu
    var = jnp.mean(d * d, axis=-1, keepdims=True)
    return d * lax.rsqrt(var + EPS) * g + b


def _outproj_kernel(oret_ref, omla_ref, x_ref, g1_ref, w_ref, lng_ref, lnb_ref, x1_ref, *, nb, rb):
    m = nb * rb
    a = oret_ref[...].reshape(m, RET_WIDTH)
    b = omla_ref[...].reshape(m, MLA_WIDTH)
    mix = _dot(a, w_ref[0:RET_WIDTH, :]) + _dot(b, w_ref[RET_WIDTH:RET_WIDTH + MLA_WIDTH, :])
    y = ALPHA * x_ref[...] + g1_ref[...] * mix.reshape(nb, rb, D_MODEL)
    x1_ref[...] = _layernorm(y, lng_ref[...], lnb_ref[...])


def _outproj(o_ret, o_mla, x, g1, w_out_b, ln_g, ln_b):
    batch, seq, _ = x.shape
    nb, rb = _row_tiling(batch, seq, ROW_TILE)
    row = lambda w: pl.BlockSpec((nb, rb, w), lambda b, i: (b, i, 0))
    mod = pl.BlockSpec((nb, 1, D_MODEL), lambda b, i: (b, 0, 0))
    return pl.pallas_call(
        functools.partial(_outproj_kernel, nb=nb, rb=rb),
        grid=(batch // nb, seq // rb),
        in_specs=[row(RET_WIDTH), row(MLA_WIDTH), row(D_MODEL), mod, _const_spec(w_out_b.shape),
                  _const_spec((1, 1, D_MODEL)), _const_spec((1, 1, D_MODEL))],
        out_specs=row(D_MODEL),
        out_shape=jax.ShapeDtypeStruct((batch, seq, D_MODEL), F32),
        compiler_params=_params(("parallel", "parallel")),
        name="out_proj_ln1",
    )(o_ret, o_mla, x, g1, w_out_b, ln_g, ln_b)


def _ffn_kernel(x1_ref, sc_ref, sh_ref, g2_ref, wg_ref, wu_ref, wd_ref, lng_ref, lnb_ref, o_ref, h_scr,
                *, nb, rb):
    m = nb * rb
    j = pl.program_id(2)

    @pl.when(j == 0)
    def _():
        x1 = x1_ref[...]
        h_scr[...] = (x1 * (1.0 + sc_ref[...]) + sh_ref[...]).astype(BF16).reshape(m, D_MODEL)
        o_ref[...] = ALPHA * x1

    h = h_scr[...]
    act = (_silu(_dot(h, wg_ref[...])) * _dot(h, wu_ref[...])).astype(BF16)
    for n in range(D_MODEL // FFN_COL_TILE):
        cols = slice(n * FFN_COL_TILE, (n + 1) * FFN_COL_TILE)
        part = _dot(act, wd_ref[:, cols]).reshape(nb, rb, FFN_COL_TILE)
        o_ref[:, :, cols] += part * g2_ref[:, :, cols]

    @pl.when(j == pl.num_programs(2) - 1)
    def _():
        o_ref[...] = _layernorm(o_ref[...], lng_ref[...], lnb_ref[...])


def _ffn(x1, sc, sh, g2, wg, wu, wd, ln_g, ln_b):
    batch, seq, _ = x1.shape
    nb, rb = _row_tiling(batch, seq, FFN_ROW_TILE)
    tf = FFN_COL_TILE
    assert D_FF % tf == 0
    row = pl.BlockSpec((nb, rb, D_MODEL), lambda b, i, j: (b, i, 0))
    mod = pl.BlockSpec((nb, 1, D_MODEL), lambda b, i, j: (b, 0, 0))
    return pl.pallas_call(
        functools.partial(_ffn_kernel, nb=nb, rb=rb),
        grid=(batch // nb, seq // rb, D_FF // tf),
        in_specs=[pl.BlockSpec((nb, rb, D_MODEL), lambda b, i, j: (b, i, 0), pipeline_mode=pl.Buffered(1)),
                  mod, mod, mod,
                  pl.BlockSpec((D_MODEL, tf), lambda b, i, j: (0, j)),
                  pl.BlockSpec((D_MODEL, tf), lambda b, i, j: (0, j)),
                  pl.BlockSpec((tf, D_MODEL), lambda b, i, j: (j, 0)),
                  _const_spec((1, 1, D_MODEL)), _const_spec((1, 1, D_MODEL))],
        out_specs=row,
        out_shape=jax.ShapeDtypeStruct((batch, seq, D_MODEL), F32),
        scratch_shapes=[pltpu.VMEM((nb * rb, D_MODEL), BF16)],
        compiler_params=_params(("parallel", "parallel", "arbitrary")),
        name="swiglu_ln2",
    )(x1, sc, sh, g2, wg, wu, wd, ln_g, ln_b)


def _layer(x, mods, pos, ckv_past, kr_past, ret_s0, wts):
    sh1, sc1, g1, sh2, sc2, g2 = mods
    cos, sin = _rope_tables(pos)
    rq, rk, rv, rg, cq, ckv, kr = _inproj(x, sc1, sh1, wts["w_in"], wts["g_cq"], wts["g_ckv"], cos, sin)
    o_ret, ret_s1 = _retention(rq, rk, rv, rg, ret_s0, wts["g_ret"], wts["b_ret"])
    if ckv_past is None:
        q, k, v = _upproj(cq, ckv, kr, cos, sin, wts["wqn"], wts["wqr"], wts["wk"], wts["wv"])
        o_mla = _flash(q, k, v)
    else:
        o_mla = _cached_attn(cq, ckv, kr, ckv_past, kr_past, cos, sin,
                             wts["wqn"], wts["wqr"], wts["wkt"], wts["wvh"])
    x1 = _outproj(o_ret, o_mla, x, g1, wts["w_out"], wts["ln1_g"], wts["ln1_b"])
    y = _ffn(x1, sc2, sh2, g2, wts["w_gate"], wts["w_up"], wts["w_down"], wts["ln2_g"], wts["ln2_b"])
    return y, ckv, kr, ret_s1


def _prep_weights(l, w_in, g_cq, g_ckv, w_uq, w_uk, w_uv, g_ret, b_ret, w_out, ln1_g, ln1_b,
                  w_gate, w_up, w_down, ln2_g, ln2_b):
    kr_lo = 2 * RET_QK + 2 * RET_WIDTH + Q_LORA + KV_LORA
    w_in_l = w_in[l]
    return {
        "w_in": jnp.concatenate([w_in_l, w_in_l[:, kr_lo:kr_lo + MLA_ROPE]], axis=1).astype(BF16),
        "g_cq": g_cq[l].reshape(1, Q_LORA),
        "g_ckv": g_ckv[l].reshape(1, KV_LORA),
        "wqn": w_uq[l][:, :, :MLA_NOPE].reshape(Q_LORA, MLA_HEADS * MLA_NOPE).astype(BF16),
        "wqr": w_uq[l][:, :, MLA_NOPE:].reshape(Q_LORA, MLA_HEADS * MLA_ROPE).astype(BF16),
        "wk": w_uk[l].reshape(KV_LORA, MLA_HEADS * MLA_NOPE).astype(BF16),
        "wv": w_uv[l].reshape(KV_LORA, MLA_HEADS * MLA_DV).astype(BF16),
        "wkt": jnp.transpose(w_uk[l], (1, 2, 0)).astype(BF16),
        "wvh": jnp.transpose(w_uv[l], (1, 0, 2)).astype(BF16),
        "g_ret": g_ret[l].reshape(1, RET_WIDTH),
        "b_ret": b_ret[l].reshape(1, RET_WIDTH),
        "w_out": w_out[l].astype(BF16),
        "ln1_g": ln1_g[l].reshape(1, 1, D_MODEL),
        "ln1_b": ln1_b[l].reshape(1, 1, D_MODEL),
        "w_gate": w_gate[l].astype(BF16),
        "w_up": w_up[l].astype(BF16),
        "w_down": w_down[l].astype(BF16),
        "ln2_g": ln2_g[l].reshape(1, 1, D_MODEL),
        "ln2_b": ln2_b[l].reshape(1, 1, D_MODEL),
    }


def kernel(x_prompt, x_sample, c_prompt, c_sample, cache_mla_ckv, cache_mla_krope, state_ret, w_ada, b_ada, w_in, g_cq, g_ckv, w_uq, w_uk, w_uv, g_ret, b_ret, w_out, ln1_g, ln1_b, w_gate, w_up, w_down, ln2_g, ln2_b):
    bp, tp, _ = x_prompt.shape
    bs, ts, _ = x_sample.shape
    past = cache_mla_ckv.shape[2]
    pos_p = jnp.arange(tp, dtype=jnp.int32)
    pos_s = past + jnp.arange(ts, dtype=jnp.int32)
    c_all = jnp.concatenate([c_prompt, c_sample], axis=0)
    yp, ys = x_prompt, x_sample
    outs = [[] for _ in range(6)]
    for l in range(w_ada.shape[0]):
        wts = _prep_weights(l, w_in, g_cq, g_ckv, w_uq, w_uk, w_uv, g_ret, b_ret, w_out, ln1_g, ln1_b,
                            w_gate, w_up, w_down, ln2_g, ln2_b)
        mod = _ada(c_all, w_ada[l], b_ada[l]).reshape(bp + bs, 6, 1, D_MODEL)
        mods_p = tuple(mod[:bp, i] for i in range(6))
        mods_s = tuple(mod[bp:, i] for i in range(6))
        s0 = jnp.zeros((bp, RET_HEADS, RET_DK, RET_DV), F32)
        yp, ckv_p, kr_p, rs_p = _layer(yp, mods_p, pos_p, None, None, s0, wts)
        ys, ckv_s, kr_s, rs_s = _layer(ys, mods_s, pos_s, cache_mla_ckv[l], cache_mla_krope[l],
                                       state_ret[l], wts)
        for lst, val in zip(outs, (ckv_p, kr_p, rs_p, ckv_s, kr_s, rs_s)):
            lst.append(val)
    stacked = [jnp.stack(v, axis=0) for v in outs]
    return (yp, ys, *stacked)
```

```python
import functools
import math

import jax
import jax.numpy as jnp
import numpy as np
from jax import lax
from jax.experimental import pallas as pl
from jax.experimental.pallas import tpu as pltpu

F32 = jnp.float32
BF16 = jnp.bfloat16

D_MODEL = 2048
CHUNK = 64
RET_HEADS = 8
RET_DK = 64
RET_DV = 128
RET_QK = RET_HEADS * RET_DK
RET_WIDTH = RET_HEADS * RET_DV
MLA_HEADS = 8
MLA_NOPE = 128
MLA_ROPE = 64
MLA_DV = 128
MLA_QK = MLA_NOPE + MLA_ROPE
MLA_WIDTH = MLA_HEADS * MLA_DV
Q_LORA = 512
KV_LORA = 512
MLA_SCALE = MLA_QK ** -0.5
D_FF = 5632
DEPTH = 1
ALPHA = (2 * DEPTH) ** 0.25
ROPE_BASE = 10000.0
EPS = 1e-5
LOG2E = math.log2(math.e)
NEG_BIG = -1e30

LANES = 128
ROW_TILE = 512
FFN_ROW_TILE = 1024
FFN_COL_TILE = 512
RET_CHUNK = 256
ATTN_TILE = 512
ATTN_HEADS_PER_STEP = 2
ADA_COL_TILE = 1024
VMEM_LIMIT = 56 * 1024 * 1024


def _params(sem, vmem=VMEM_LIMIT):
    return pltpu.CompilerParams(dimension_semantics=sem, vmem_limit_bytes=vmem)


def _const_spec(shape):
    zeros = (0,) * len(shape)
    return pl.BlockSpec(shape, lambda *_: zeros, pipeline_mode=pl.Buffered(1))


def _dot(a, b):
    return jnp.dot(a, b, preferred_element_type=F32)


def _dot_nt(a, b):
    return lax.dot_general(a, b, (((1,), (1,)), ((), ())), preferred_element_type=F32)


def _dot_tn(a, b):
    return lax.dot_general(a, b, (((0,), (0,)), ((), ())), preferred_element_type=F32)


def _silu(x):
    return x * jax.nn.sigmoid(x)


def _row_tiling(batch, seq, rows):
    if seq >= rows:
        assert seq % rows == 0
        return 1, rows
    nb = min(batch, rows // seq)
    assert batch % nb == 0
    return nb, seq


def _rope_cols(y, cos, sin, first_half):
    outs = []
    for g in range(y.shape[1] // LANES):
        yg = y[:, g * LANES:(g + 1) * LANES]
        rot = jnp.where(first_half, pltpu.roll(yg, LANES - 32, 1), pltpu.roll(yg, 32, 1))
        outs.append(yg * cos + rot * sin)
    return outs[0] if len(outs) == 1 else jnp.concatenate(outs, axis=1)


def _rope_tables(pos):
    half = 32
    inv = ROPE_BASE ** (-jnp.arange(half, dtype=F32) / half)
    ang = pos.astype(F32)[:, None] * inv[None, :]
    cos = jnp.cos(ang)
    sin = jnp.sin(ang)
    cos128 = jnp.concatenate([cos, cos, cos, cos], axis=1)
    sin128 = jnp.concatenate([-sin, sin, -sin, sin], axis=1)
    return cos128, sin128


def _rope_tables_t(pos):
    half = 32
    inv = ROPE_BASE ** (-jnp.arange(half, dtype=F32) / half)
    ang = pos.astype(F32)[:, None] * inv[None, :]
    return jnp.transpose(jnp.cos(ang)), jnp.transpose(jnp.sin(ang))


def _tile_rows(t, nb):
    if nb == 1:
        return t
    return jnp.broadcast_to(t[None], (nb,) + t.shape).reshape(nb * t.shape[0], t.shape[1])


def _ada_kernel(c_ref, w_ref, b_ref, o_ref):
    c = c_ref[...]
    a = _silu(c).astype(BF16)
    o_ref[...] = _dot(a, w_ref[...].astype(BF16)) + b_ref[...]


def _ada(c, w_ada, b_ada):
    nrow = c.shape[0]
    ncol = w_ada.shape[1]
    tn = ADA_COL_TILE
    return pl.pallas_call(
        _ada_kernel,
        grid=(ncol // tn,),
        in_specs=[_const_spec((nrow, D_MODEL)),
                  pl.BlockSpec((D_MODEL, tn), lambda j: (0, j)),
                  pl.BlockSpec((1, tn), lambda j: (0, j))],
        out_specs=pl.BlockSpec((nrow, tn), lambda j: (0, j)),
        out_shape=jax.ShapeDtypeStruct((nrow, ncol), F32),
        compiler_params=_params(("arbitrary",)),
        name="ada_mod",
    )(c, w_ada, b_ada.reshape(1, ncol))


IN_W = 2 * RET_QK + 2 * RET_WIDTH + Q_LORA + KV_LORA + 2 * MLA_ROPE


def _inproj_kernel(x_ref, sc_ref, sh_ref, w_ref, gcq_ref, gckv_ref, cos_ref, sin_ref,
                   rq_ref, rk_ref, rv_ref, rg_ref, cq_ref, ckv_ref, kr_ref, *, nb, rb):
    m = nb * rb
    h = x_ref[...] * (1.0 + sc_ref[...]) + sh_ref[...]
    hb = h.astype(BF16).reshape(m, D_MODEL)
    cos = _tile_rows(cos_ref[...], nb)
    sin = _tile_rows(sin_ref[...], nb)
    lane = lax.broadcasted_iota(jnp.int32, (m, LANES), 1)
    first_half = (lane % 64) < 32

    def proj(lo, hi):
        return _dot(hb, w_ref[:, lo:hi])

    def rmsnorm(y, g):
        ms = jnp.mean(y * y, axis=-1, keepdims=True)
        return y * lax.rsqrt(ms + EPS) * g

    o = 0
    rq = _rope_cols(proj(o, o + RET_QK), cos, sin, first_half) * (RET_DK ** -0.5)
    rq_ref[...] = rq.astype(BF16).reshape(nb, rb, RET_QK)
    o += RET_QK
    rk = _rope_cols(proj(o, o + RET_QK), cos, sin, first_half)
    rk_ref[...] = rk.astype(BF16).reshape(nb, rb, RET_QK)
    o += RET_QK
    rv_ref[...] = proj(o, o + RET_WIDTH).astype(BF16).reshape(nb, rb, RET_WIDTH)
    o += RET_WIDTH
    rg_ref[...] = proj(o, o + RET_WIDTH).astype(BF16).reshape(nb, rb, RET_WIDTH)
    o += RET_WIDTH
    cq_ref[...] = rmsnorm(proj(o, o + Q_LORA), gcq_ref[...]).astype(BF16).reshape(nb, rb, Q_LORA)
    o += Q_LORA
    ckv_ref[...] = rmsnorm(proj(o, o + KV_LORA), gckv_ref[...]).reshape(nb, rb, KV_LORA)
    o += KV_LORA
    kr = _rope_cols(proj(o, o + LANES), cos, sin, first_half)
    kr_ref[...] = kr[:, :MLA_ROPE].reshape(nb, rb, MLA_ROPE)


def _inproj(x, sc, sh, w_in_b, g_cq, g_ckv, cos, sin):
    batch, seq, _ = x.shape
    nb, rb = _row_tiling(batch, seq, ROW_TILE)
    grid = (batch // nb, seq // rb)
    row = lambda w: pl.BlockSpec((nb, rb, w), lambda b, i: (b, i, 0))
    mod = pl.BlockSpec((nb, 1, D_MODEL), lambda b, i: (b, 0, 0))
    tab = pl.BlockSpec((rb, LANES), lambda b, i: (i, 0))
    sds = lambda w, dt: jax.ShapeDtypeStruct((batch, seq, w), dt)
    return pl.pallas_call(
        functools.partial(_inproj_kernel, nb=nb, rb=rb),
        grid=grid,
        in_specs=[row(D_MODEL), mod, mod, _const_spec((D_MODEL, IN_W)),
                  _const_spec((1, Q_LORA)), _const_spec((1, KV_LORA)), tab, tab],
        out_specs=[row(RET_QK), row(RET_QK), row(RET_WIDTH), row(RET_WIDTH),
                   row(Q_LORA), row(KV_LORA), row(MLA_ROPE)],
        out_shape=[sds(RET_QK, BF16), sds(RET_QK, BF16), sds(RET_WIDTH, BF16), sds(RET_WIDTH, BF16),
                   sds(Q_LORA, BF16), sds(KV_LORA, F32), sds(MLA_ROPE, F32)],
        compiler_params=_params(("parallel", "parallel")),
        name="in_proj",
    )(x, sc, sh, w_in_b, g_cq, g_ckv, cos, sin)


def _ret_tables(c):
    hh = jnp.arange(RET_HEADS, dtype=F32)
    logg = jnp.log1p(-jnp.exp2(-5.0 - hh))
    i = jnp.arange(c, dtype=F32)
    diff = i[:, None] - i[None, :]
    dmask = jnp.where(diff[None] >= 0.0, jnp.exp(jnp.maximum(diff, 0.0)[None] * logg[:, None, None]), 0.0)
    qdec = jnp.exp((i + 1.0)[:, None] * logg[None, :])
    kdec = jnp.exp((c - 1.0 - i)[:, None] * logg[None, :])
    cdec = jnp.exp(c * logg)
    qdec = jnp.repeat(qdec, RET_DK, axis=1)
    kdec = jnp.repeat(kdec, RET_DK, axis=1)
    r = jnp.arange(2 * RET_DK)[:, None] // RET_DK
    col = jnp.arange(2 * RET_DV)[None, :] // RET_DV
    pair = jnp.arange(RET_HEADS // 2)[:, None, None]
    sdec = jnp.where(r[None] == col[None], cdec[2 * pair + r[None]], 0.0)
    return dmask.astype(F32), qdec.astype(F32), kdec.astype(F32), sdec.astype(F32)


def _ret_kernel(rq_ref, rk_ref, rv_ref, rg_ref, s0_ref, dmask_ref, qdec_ref, kdec_ref, sdec_ref,
                gret_ref, bret_ref, o_ref, so_ref, s_scr, *, c):
    n = pl.program_id(1)
    npair = RET_HEADS // 2

    @pl.when(n == 0)
    def _():
        s_scr[...] = jnp.zeros(s_scr.shape, F32)
        for p in range(npair):
            s_scr[p, 0:RET_DK, 0:RET_DV] = s0_ref[0, 2 * p]
            s_scr[p, RET_DK:2 * RET_DK, RET_DV:2 * RET_DV] = s0_ref[0, 2 * p + 1]

    q = rq_ref[0]
    k = rk_ref[0]
    qd = (q.astype(F32) * qdec_ref[...]).astype(BF16)
    kd = (k.astype(F32) * kdec_ref[...]).astype(BF16)
    lane = lax.broadcasted_iota(jnp.int32, (c, LANES), 1)
    lo = lane < RET_DK
    zero = jnp.zeros((c, LANES), BF16)
    for p in range(npair):
        qp = q[:, p * LANES:(p + 1) * LANES]
        kp = k[:, p * LANES:(p + 1) * LANES]
        vp = rv_ref[0, :, p * 2 * RET_DV:(p + 1) * 2 * RET_DV]
        sc0 = _dot_nt(jnp.where(lo, qp, zero), kp) * dmask_ref[2 * p]
        sc1 = _dot_nt(jnp.where(lo, zero, qp), kp) * dmask_ref[2 * p + 1]
        intra0 = _dot(sc0.astype(BF16), vp[:, :RET_DV])
        intra1 = _dot(sc1.astype(BF16), vp[:, RET_DV:])
        s = s_scr[p]
        inter = _dot(qd[:, p * LANES:(p + 1) * LANES], s.astype(BF16))
        kv = _dot_tn(kd[:, p * LANES:(p + 1) * LANES], vp)
        sdec = sdec_ref[p]
        s_scr[p] = s * sdec + jnp.where(sdec > 0.0, kv, 0.0)
        for hh in range(2):
            hd = 2 * p + hh
            cols = slice(hd * RET_DV, (hd + 1) * RET_DV)
            oh = (intra0 if hh == 0 else intra1) + inter[:, hh * RET_DV:(hh + 1) * RET_DV]
            mu = jnp.mean(oh, axis=-1, keepdims=True)
            d = oh - mu
            var = jnp.mean(d * d, axis=-1, keepdims=True)
            y = d * lax.rsqrt(var + EPS) * gret_ref[:, cols] + bret_ref[:, cols]
            gate = rg_ref[0, :, cols].astype(F32)
            o_ref[0, :, cols] = (_silu(gate) * y).astype(BF16)

    @pl.when(n == pl.num_programs(1) - 1)
    def _():
        for p in range(npair):
            so_ref[0, 2 * p] = s_scr[p, 0:RET_DK, 0:RET_DV]
            so_ref[0, 2 * p + 1] = s_scr[p, RET_DK:2 * RET_DK, RET_DV:2 * RET_DV]


def _retention(rq, rk, rv, rg, s0, g_ret, b_ret):
    batch, seq, _ = rq.shape
    c = min(RET_CHUNK, seq)
    assert seq % c == 0
    dmask, qdec, kdec, sdec = _ret_tables(c)
    row = lambda w: pl.BlockSpec((1, c, w), lambda b, n: (b, n, 0))
    st = pl.BlockSpec((1, RET_HEADS, RET_DK, RET_DV), lambda b, n: (b, 0, 0, 0))
    return pl.pallas_call(
        functools.partial(_ret_kernel, c=c),
        grid=(batch, seq // c),
        in_specs=[row(RET_QK), row(RET_QK), row(RET_WIDTH), row(RET_WIDTH), st,
                  _const_spec(dmask.shape), _const_spec(qdec.shape), _const_spec(kdec.shape),
                  _const_spec(sdec.shape), _const_spec((1, RET_WIDTH)), _const_spec((1, RET_WIDTH))],
        out_specs=[row(RET_WIDTH), st],
        out_shape=[jax.ShapeDtypeStruct((batch, seq, RET_WIDTH), BF16),
                   jax.ShapeDtypeStruct((batch, RET_HEADS, RET_DK, RET_DV), F32)],
        scratch_shapes=[pltpu.VMEM((RET_HEADS // 2, 2 * RET_DK, 2 * RET_DV), F32)],
        compiler_params=_params(("parallel", "arbitrary")),
        name="retention",
    )(rq, rk, rv, rg, s0, dmask, qdec, kdec, sdec, g_ret, b_ret)


def _upproj_kernel(cq_ref, ckv_ref, kr_ref, cos_ref, sin_ref, wqt_ref, wk_ref, wvt_ref,
                   qt_ref, k_ref, vt_ref):
    cq = cq_ref[0]
    ckv = ckv_ref[0].astype(BF16)
    krb = kr_ref[0].astype(BF16)
    cos = cos_ref[...]
    sin = sin_ref[...]
    half = MLA_ROPE // 2
    qt = _dot_nt(wqt_ref[...], cq) * (MLA_SCALE * LOG2E)
    kn = _dot(ckv, wk_ref[...]).astype(BF16)
    vt = _dot_nt(wvt_ref[...], ckv).astype(BF16)
    for hd in range(MLA_HEADS):
        r0 = hd * MLA_QK
        x1 = qt[r0 + MLA_NOPE:r0 + MLA_NOPE + half]
        x2 = qt[r0 + MLA_NOPE + half:r0 + MLA_QK]
        qt_ref[0, hd, 0:MLA_NOPE] = qt[r0:r0 + MLA_NOPE].astype(BF16)
        qt_ref[0, hd, MLA_NOPE:MLA_NOPE + half] = (x1 * cos - x2 * sin).astype(BF16)
        qt_ref[0, hd, MLA_NOPE + half:MLA_QK] = (x2 * cos + x1 * sin).astype(BF16)
        k_ref[0, hd, :, 0:MLA_NOPE] = kn[:, hd * MLA_NOPE:(hd + 1) * MLA_NOPE]
        k_ref[0, hd, :, MLA_NOPE:MLA_QK] = krb
        vt_ref[0, hd] = vt[hd * MLA_DV:(hd + 1) * MLA_DV]


def _upproj(cq, ckv, kr, cos_t, sin_t, wqt, wk, wvt):
    batch, seq, _ = cq.shape
    rb = min(ROW_TILE, seq)
    assert seq % rb == 0
    row = lambda w: pl.BlockSpec((1, rb, w), lambda b, i: (b, i, 0))
    tab = pl.BlockSpec((MLA_ROPE // 2, rb), lambda b, i: (0, i))
    head_t = lambda w: pl.BlockSpec((1, MLA_HEADS, w, rb), lambda b, i: (b, 0, 0, i))
    return pl.pallas_call(
        _upproj_kernel,
        grid=(batch, seq // rb),
        in_specs=[row(Q_LORA), row(KV_LORA), row(MLA_ROPE), tab, tab,
                  _const_spec(wqt.shape), _const_spec(wk.shape), _const_spec(wvt.shape)],
        out_specs=[head_t(MLA_QK),
                   pl.BlockSpec((1, MLA_HEADS, rb, MLA_QK), lambda b, i: (b, 0, i, 0)),
                   head_t(MLA_DV)],
        out_shape=[jax.ShapeDtypeStruct((batch, MLA_HEADS, MLA_QK, seq), BF16),
                   jax.ShapeDtypeStruct((batch, MLA_HEADS, seq, MLA_QK), BF16),
                   jax.ShapeDtypeStruct((batch, MLA_HEADS, MLA_DV, seq), BF16)],
        compiler_params=_params(("parallel", "parallel")),
        name="mla_up_proj",
    )(cq, ckv, kr, cos_t, sin_t, wqt, wk, wvt)


def _flash_kernel(qt_ref, k_ref, vt_ref, o_ref, m_scr, l_scr, acc_scr, *, t):
    i = pl.program_id(2)
    m_scr[...] = jnp.full(m_scr.shape, -jnp.inf, F32)
    l_scr[...] = jnp.zeros(l_scr.shape, F32)
    acc_scr[...] = jnp.zeros(acc_scr.shape, F32)

    def block(j, masked):
        start = pl.multiple_of(j * t, t)
        for hd in range(ATTN_HEADS_PER_STEP):
            kb = k_ref[0, hd, pl.ds(start, t), :]
            vtb = vt_ref[0, hd, :, pl.ds(start, t)]
            s = _dot(kb, qt_ref[0, hd])
            if masked:
                kc = lax.broadcasted_iota(jnp.int32, (t, t), 0) // CHUNK
                qc = lax.broadcasted_iota(jnp.int32, (t, t), 1) // CHUNK
                s = jnp.where(kc <= qc, s, NEG_BIG)
            m_prev = m_scr[hd]
            m_new = jnp.maximum(m_prev, jnp.max(s, axis=0, keepdims=True))
            alpha = jnp.exp2(m_prev - m_new)
            p = jnp.exp2(s - m_new)
            l_scr[hd] = alpha * l_scr[hd] + jnp.sum(p, axis=0, keepdims=True)
            acc_scr[hd] = alpha * acc_scr[hd] + _dot(vtb, p.astype(BF16))
            m_scr[hd] = m_new

    def body(j, carry):
        block(j, False)
        return carry

    lax.fori_loop(0, i, body, 0)
    block(i, True)
    for hd in range(ATTN_HEADS_PER_STEP):
        o_ref[0, :, hd * MLA_DV:(hd + 1) * MLA_DV] = jnp.transpose(acc_scr[hd] / l_scr[hd]).astype(BF16)


def _flash(qt, k, vt):
    batch, heads, seq, _ = k.shape
    t = min(ATTN_TILE, seq)
    g = ATTN_HEADS_PER_STEP
    assert seq % t == 0 and t % CHUNK == 0 and heads % g == 0
    return pl.pallas_call(
        functools.partial(_flash_kernel, t=t),
        grid=(batch, heads // g, seq // t),
        in_specs=[pl.BlockSpec((1, g, MLA_QK, t), lambda b, h, i: (b, h, 0, i)),
                  pl.BlockSpec((1, g, seq, MLA_QK), lambda b, h, i: (b, h, 0, 0)),
                  pl.BlockSpec((1, g, MLA_DV, seq), lambda b, h, i: (b, h, 0, 0))],
        out_specs=pl.BlockSpec((1, t, g * MLA_DV), lambda b, h, i: (b, i, h)),
        out_shape=jax.ShapeDtypeStruct((batch, seq, MLA_WIDTH), BF16),
        scratch_shapes=[pltpu.VMEM((g, 1, t), F32), pltpu.VMEM((g, 1, t), F32),
                        pltpu.VMEM((g, MLA_DV, t), F32)],
        compiler_params=_params(("parallel", "parallel", "arbitrary")),
        name="mla_flash",
    )(qt, k, vt)


def _cached_attn_kernel(cq_ref, ckvn_ref, krn_ref, ckvp_ref, krp_ref, cos_ref, sin_ref,
                        wqn_ref, wqr_ref, wkt_ref, wv_ref, o_ref, *, tq, past):
    cq = cq_ref[0]
    lane = lax.broadcasted_iota(jnp.int32, (tq, LANES), 1)
    first_half = (lane % 64) < 32
    qscale = MLA_SCALE * LOG2E
    qn = (_dot(cq, wqn_ref[...]) * qscale).astype(BF16)
    qr = (_rope_cols(_dot(cq, wqr_ref[...]), cos_ref[...], sin_ref[...], first_half) * qscale).astype(BF16)
    qlat = jnp.concatenate(
        [_dot(qn[:, hd * MLA_NOPE:(hd + 1) * MLA_NOPE], wkt_ref[hd]) for hd in range(MLA_HEADS)],
        axis=0).astype(BF16)
    qrr = jnp.concatenate([qr[:, hd * MLA_ROPE:(hd + 1) * MLA_ROPE] for hd in range(MLA_HEADS)], axis=0)
    ckv_p = ckvp_ref[0].astype(BF16)
    kr_p = krp_ref[0].astype(BF16)
    ckv_n = ckvn_ref[0].astype(BF16)
    kr_n = krn_ref[0].astype(BF16)
    s_p = _dot_nt(qlat, ckv_p) + _dot_nt(qrr, kr_p)
    s_n = _dot_nt(qlat, ckv_n) + _dot_nt(qrr, kr_n)
    rows = MLA_HEADS * tq
    qpos = past + lax.broadcasted_iota(jnp.int32, (rows, tq), 0) % tq
    kpos = past + lax.broadcasted_iota(jnp.int32, (rows, tq), 1)
    s_n = jnp.where(kpos // CHUNK <= qpos // CHUNK, s_n, NEG_BIG)
    m = jnp.maximum(jnp.max(s_p, axis=1, keepdims=True), jnp.max(s_n, axis=1, keepdims=True))
    p_p = jnp.exp2(s_p - m)
    p_n = jnp.exp2(s_n - m)
    l = jnp.sum(p_p, axis=1, keepdims=True) + jnp.sum(p_n, axis=1, keepdims=True)
    olat = (_dot(p_p.astype(BF16), ckv_p) + _dot(p_n.astype(BF16), ckv_n)) / l
    olat = olat.astype(BF16)
    for hd in range(MLA_HEADS):
        o_ref[0, :, hd * MLA_DV:(hd + 1) * MLA_DV] = _dot(olat[hd * tq:(hd + 1) * tq], wv_ref[hd]).astype(BF16)


def _cached_attn(cq, ckv_new, kr_new, ckv_past, kr_past, cos, sin, wqn, wqr, wkt, wvh):
    batch, tq, _ = cq.shape
    past = ckv_past.shape[1]
    assert past % CHUNK == 0
    one = lambda n, w: pl.BlockSpec((1, n, w), lambda b: (b, 0, 0))
    return pl.pallas_call(
        functools.partial(_cached_attn_kernel, tq=tq, past=past),
        grid=(batch,),
        in_specs=[one(tq, Q_LORA), one(tq, KV_LORA), one(tq, MLA_ROPE), one(past, KV_LORA), one(past, MLA_ROPE),
                  _const_spec(cos.shape), _const_spec(sin.shape), _const_spec(wqn.shape), _const_spec(wqr.shape),
                  _const_spec(wkt.shape), _const_spec(wvh.shape)],
        out_specs=one(tq, MLA_WIDTH),
        out_shape=jax.ShapeDtypeStruct((batch, tq, MLA_WIDTH), BF16),
        compiler_params=_params(("parallel",)),
        name="mla_cached",
    )(cq, ckv_new, kr_new, ckv_past, kr_past, cos, sin, wqn, wqr, wkt, wvh)


def _layernorm(y, g, b):
    mu = jnp.mean(y, axis=-1, keepdims=True)
    d = y - mu
    var = jnp.mean(d * d, axis=-1, keepdims=True)
    return d * lax.rsqrt(var + EPS) * g + b


def _outproj_kernel(oret_ref, omla_ref, x_ref, g1_ref, w_ref, lng_ref, lnb_ref, x1_ref, *, nb, rb):
    m = nb * rb
    a = oret_ref[...].reshape(m, RET_WIDTH)
    b = omla_ref[...].reshape(m, MLA_WIDTH)
    mix = _dot(a, w_ref[0:RET_WIDTH, :]) + _dot(b, w_ref[RET_WIDTH:RET_WIDTH + MLA_WIDTH, :])
    y = ALPHA * x_ref[...] + g1_ref[...] * mix.reshape(nb, rb, D_MODEL)
    x1_ref[...] = _layernorm(y, lng_ref[...], lnb_ref[...])


def _outproj(o_ret, o_mla, x, g1, w_out_b, ln_g, ln_b):
    batch, seq, _ = x.shape
    nb, rb = _row_tiling(batch, seq, ROW_TILE)
    row = lambda w: pl.BlockSpec((nb, rb, w), lambda b, i: (b, i, 0))
    mod = pl.BlockSpec((nb, 1, D_MODEL), lambda b, i: (b, 0, 0))
    return pl.pallas_call(
        functools.partial(_outproj_kernel, nb=nb, rb=rb),
        grid=(batch // nb, seq // rb),
        in_specs=[row(RET_WIDTH), row(MLA_WIDTH), row(D_MODEL), mod, _const_spec(w_out_b.shape),
                  _const_spec((1, 1, D_MODEL)), _const_spec((1, 1, D_MODEL))],
        out_specs=row(D_MODEL),
        out_shape=jax.ShapeDtypeStruct((batch, seq, D_MODEL), F32),
        compiler_params=_params(("parallel", "parallel")),
        name="out_proj_ln1",
    )(o_ret, o_mla, x, g1, w_out_b, ln_g, ln_b)


def _ffn_kernel(x1_ref, sc_ref, sh_ref, g2_ref, wg_ref, wu_ref, wd_ref, lng_ref, lnb_ref, o_ref, h_scr,
                *, nb, rb):
    m = nb * rb
    j = pl.program_id(2)

    @pl.when(j == 0)
    def _():
        x1 = x1_ref[...]
        h_scr[...] = (x1 * (1.0 + sc_ref[...]) + sh_ref[...]).astype(BF16).reshape(m, D_MODEL)
        o_ref[...] = ALPHA * x1

    h = h_scr[...]
    act = (_silu(_dot(h, wg_ref[...])) * _dot(h, wu_ref[...])).astype(BF16)
    for n in range(D_MODEL // FFN_COL_TILE):
        cols = slice(n * FFN_COL_TILE, (n + 1) * FFN_COL_TILE)
        part = _dot(act, wd_ref[:, cols]).reshape(nb, rb, FFN_COL_TILE)
        o_ref[:, :, cols] += part * g2_ref[:, :, cols]

    @pl.when(j == pl.num_programs(2) - 1)
    def _():
        o_ref[...] = _layernorm(o_ref[...], lng_ref[...], lnb_ref[...])


def _ffn(x1, sc, sh, g2, wg, wu, wd, ln_g, ln_b):
    batch, seq, _ = x1.shape
    nb, rb = _row_tiling(batch, seq, FFN_ROW_TILE)
    tf = FFN_COL_TILE
    assert D_FF % tf == 0
    row = pl.BlockSpec((nb, rb, D_MODEL), lambda b, i, j: (b, i, 0))
    mod = pl.BlockSpec((nb, 1, D_MODEL), lambda b, i, j: (b, 0, 0))
    return pl.pallas_call(
        functools.partial(_ffn_kernel, nb=nb, rb=rb),
        grid=(batch // nb, seq // rb, D_FF // tf),
        in_specs=[pl.BlockSpec((nb, rb, D_MODEL), lambda b, i, j: (b, i, 0), pipeline_mode=pl.Buffered(1)),
                  mod, mod, mod,
                  pl.BlockSpec((D_MODEL, tf), lambda b, i, j: (0, j)),
                  pl.BlockSpec((D_MODEL, tf), lambda b, i, j: (0, j)),
                  pl.BlockSpec((tf, D_MODEL), lambda b, i, j: (j, 0)),
                  _const_spec((1, 1, D_MODEL)), _const_spec((1, 1, D_MODEL))],
        out_specs=row,
        out_shape=jax.ShapeDtypeStruct((batch, seq, D_MODEL), F32),
        scratch_shapes=[pltpu.VMEM((nb * rb, D_MODEL), BF16)],
        compiler_params=_params(("parallel", "parallel", "arbitrary")),
        name="swiglu_ln2",
    )(x1, sc, sh, g2, wg, wu, wd, ln_g, ln_b)


def _layer(x, mods, pos, ckv_past, kr_past, ret_s0, wts):
    sh1, sc1, g1, sh2, sc2, g2 = mods
    cos, sin = _rope_tables(pos)
    rq, rk, rv, rg, cq, ckv, kr = _inproj(x, sc1, sh1, wts["w_in"], wts["g_cq"], wts["g_ckv"], cos, sin)
    o_ret, ret_s1 = _retention(rq, rk, rv, rg, ret_s0, wts["g_ret"], wts["b_ret"])
    if ckv_past is None:
        cos_t, sin_t = _rope_tables_t(pos)
        qt, k, vt = _upproj(cq, ckv, kr, cos_t, sin_t, wts["wqt"], wts["wk"], wts["wvt"])
        o_mla = _flash(qt, k, vt)
    else:
        o_mla = _cached_attn(cq, ckv, kr, ckv_past, kr_past, cos, sin,
                             wts["wqn"], wts["wqr"], wts["wkt"], wts["wvh"])
    x1 = _outproj(o_ret, o_mla, x, g1, wts["w_out"], wts["ln1_g"], wts["ln1_b"])
    y = _ffn(x1, sc2, sh2, g2, wts["w_gate"], wts["w_up"], wts["w_down"], wts["ln2_g"], wts["ln2_b"])
    return y, ckv, kr, ret_s1


def _prep_weights(l, w_in, g_cq, g_ckv, w_uq, w_uk, w_uv, g_ret, b_ret, w_out, ln1_g, ln1_b,
                  w_gate, w_up, w_down, ln2_g, ln2_b):
    kr_lo = 2 * RET_QK + 2 * RET_WIDTH + Q_LORA + KV_LORA
    w_in_l = w_in[l]
    return {
        "w_in": jnp.concatenate([w_in_l, w_in_l[:, kr_lo:kr_lo + MLA_ROPE]], axis=1).astype(BF16),
        "g_cq": g_cq[l].reshape(1, Q_LORA),
        "g_ckv": g_ckv[l].reshape(1, KV_LORA),
        "wqn": w_uq[l][:, :, :MLA_NOPE].reshape(Q_LORA, MLA_HEADS * MLA_NOPE).astype(BF16),
        "wqr": w_uq[l][:, :, MLA_NOPE:].reshape(Q_LORA, MLA_HEADS * MLA_ROPE).astype(BF16),
        "wk": w_uk[l].reshape(KV_LORA, MLA_HEADS * MLA_NOPE).astype(BF16),
        "wqt": jnp.transpose(w_uq[l].reshape(Q_LORA, MLA_HEADS * MLA_QK)).astype(BF16),
        "wvt": jnp.transpose(w_uv[l].reshape(KV_LORA, MLA_HEADS * MLA_DV)).astype(BF16),
        "wkt": jnp.transpose(w_uk[l], (1, 2, 0)).astype(BF16),
        "wvh": jnp.transpose(w_uv[l], (1, 0, 2)).astype(BF16),
        "g_ret": g_ret[l].reshape(1, RET_WIDTH),
        "b_ret": b_ret[l].reshape(1, RET_WIDTH),
        "w_out": w_out[l].astype(BF16),
        "ln1_g": ln1_g[l].reshape(1, 1, D_MODEL),
        "ln1_b": ln1_b[l].reshape(1, 1, D_MODEL),
        "w_gate": w_gate[l].astype(BF16),
        "w_up": w_up[l].astype(BF16),
        "w_down": w_down[l].astype(BF16),
        "ln2_g": ln2_g[l].reshape(1, 1, D_MODEL),
        "ln2_b": ln2_b[l].reshape(1, 1, D_MODEL),
    }


def kernel(x_prompt, x_sample, c_prompt, c_sample, cache_mla_ckv, cache_mla_krope, state_ret, w_ada, b_ada, w_in, g_cq, g_ckv, w_uq, w_uk, w_uv, g_ret, b_ret, w_out, ln1_g, ln1_b, w_gate, w_up, w_down, ln2_g, ln2_b):
    bp, tp, _ = x_prompt.shape
    bs, ts, _ = x_sample.shape
    past = cache_mla_ckv.shape[2]
    pos_p = jnp.arange(tp, dtype=jnp.int32)
    pos_s = past + jnp.arange(ts, dtype=jnp.int32)
    c_all = jnp.concatenate([c_prompt, c_sample], axis=0)
    yp, ys = x_prompt, x_sample
    outs = [[] for _ in range(6)]
    for l in range(w_ada.shape[0]):
        wts = _prep_weights(l, w_in, g_cq, g_ckv, w_uq, w_uk, w_uv, g_ret, b_ret, w_out, ln1_g, ln1_b,
                            w_gate, w_up, w_down, ln2_g, ln2_b)
        mod = _ada(c_all, w_ada[l], b_ada[l]).reshape(bp + bs, 6, 1, D_MODEL)
        mods_p = tuple(mod[:bp, i] for i in range(6))
        mods_s = tuple(mod[bp:, i] for i in range(6))
        s0 = jnp.zeros((bp, RET_HEADS, RET_DK, RET_DV), F32)
        yp, ckv_p, kr_p, rs_p = _layer(yp, mods_p, pos_p, None, None, s0, wts)
        ys, ckv_s, kr_s, rs_s = _layer(ys, mods_s, pos_s, cache_mla_ckv[l], cache_mla_krope[l],
                                       state_ret[l], wts)
        for lst, val in zip(outs, (ckv_p, kr_p, rs_p, ckv_s, kr_s, rs_s)):
            lst.append(val)
    stacked = [jnp.stack(v, axis=0) for v in outs]
    return (yp, ys, *stacked)
```

```python
import functools
import math

import jax
import jax.numpy as jnp
import numpy as np
from jax import lax
from jax.experimental import pallas as pl
from jax.experimental.pallas import tpu as pltpu

F32 = jnp.float32
BF16 = jnp.bfloat16

D_MODEL = 2048
CHUNK = 64
RET_HEADS = 8
RET_DK = 64
RET_DV = 128
RET_QK = RET_HEADS * RET_DK
RET_WIDTH = RET_HEADS * RET_DV
MLA_HEADS = 8
MLA_NOPE = 128
MLA_ROPE = 64
MLA_DV = 128
MLA_QK = MLA_NOPE + MLA_ROPE
MLA_WIDTH = MLA_HEADS * MLA_DV
Q_LORA = 512
KV_LORA = 512
MLA_SCALE = MLA_QK ** -0.5
D_FF = 5632
DEPTH = 1
ALPHA = (2 * DEPTH) ** 0.25
ROPE_BASE = 10000.0
EPS = 1e-5
LOG2E = math.log2(math.e)
NEG_BIG = -1e30

LANES = 128
ROW_TILE = 512
FFN_ROW_TILE = 1024
FFN_COL_TILE = 512
RET_CHUNK = 256
ATTN_TILE = 512
ATTN_HEADS_PER_STEP = 2
ADA_COL_TILE = 1024
VMEM_LIMIT = 56 * 1024 * 1024


def _params(sem, vmem=VMEM_LIMIT):
    return pltpu.CompilerParams(dimension_semantics=sem, vmem_limit_bytes=vmem)


def _const_spec(shape):
    zeros = (0,) * len(shape)
    return pl.BlockSpec(shape, lambda *_: zeros, pipeline_mode=pl.Buffered(1))


def _dot(a, b):
    return jnp.dot(a, b, preferred_element_type=F32)


def _dot_nt(a, b):
    return lax.dot_general(a, b, (((1,), (1,)), ((), ())), preferred_element_type=F32)


def _dot_tn(a, b):
    return lax.dot_general(a, b, (((0,), (0,)), ((), ())), preferred_element_type=F32)


def _silu(x):
    return x * jax.nn.sigmoid(x)


def _row_tiling(batch, seq, rows):
    if seq >= rows:
        assert seq % rows == 0
        return 1, rows
    nb = min(batch, rows // seq)
    assert batch % nb == 0
    return nb, seq


def _rope_cols(y, cos, sin, first_half):
    outs = []
    for g in range(y.shape[1] // LANES):
        yg = y[:, g * LANES:(g + 1) * LANES]
        rot = jnp.where(first_half, pltpu.roll(yg, LANES - 32, 1), pltpu.roll(yg, 32, 1))
        outs.append(yg * cos + rot * sin)
    return outs[0] if len(outs) == 1 else jnp.concatenate(outs, axis=1)


def _rope_tables(pos):
    half = 32
    inv = ROPE_BASE ** (-jnp.arange(half, dtype=F32) / half)
    ang = pos.astype(F32)[:, None] * inv[None, :]
    cos = jnp.cos(ang)
    sin = jnp.sin(ang)
    cos128 = jnp.concatenate([cos, cos, cos, cos], axis=1)
    sin128 = jnp.concatenate([-sin, sin, -sin, sin], axis=1)
    return cos128, sin128


def _rope_tables_t(pos):
    half = 32
    inv = ROPE_BASE ** (-jnp.arange(half, dtype=F32) / half)
    ang = pos.astype(F32)[:, None] * inv[None, :]
    return jnp.transpose(jnp.cos(ang)), jnp.transpose(jnp.sin(ang))


def _tile_rows(t, nb):
    if nb == 1:
        return t
    return jnp.broadcast_to(t[None], (nb,) + t.shape).reshape(nb * t.shape[0], t.shape[1])


def _ada_kernel(c_ref, w_ref, b_ref, o_ref):
    c = c_ref[...]
    a = _silu(c).astype(BF16)
    o_ref[...] = _dot(a, w_ref[...].astype(BF16)) + b_ref[...]


def _ada(c, w_ada, b_ada):
    nrow = c.shape[0]
    ncol = w_ada.shape[1]
    tn = ADA_COL_TILE
    return pl.pallas_call(
        _ada_kernel,
        grid=(ncol // tn,),
        in_specs=[_const_spec((nrow, D_MODEL)),
                  pl.BlockSpec((D_MODEL, tn), lambda j: (0, j)),
                  pl.BlockSpec((1, tn), lambda j: (0, j))],
        out_specs=pl.BlockSpec((nrow, tn), lambda j: (0, j)),
        out_shape=jax.ShapeDtypeStruct((nrow, ncol), F32),
        compiler_params=_params(("arbitrary",)),
        name="ada_mod",
    )(c, w_ada, b_ada.reshape(1, ncol))


IN_W = 2 * RET_QK + 2 * RET_WIDTH + Q_LORA + KV_LORA + 2 * MLA_ROPE


def _inproj_kernel(x_ref, sc_ref, sh_ref, w_ref, gcq_ref, gckv_ref, cos_ref, sin_ref,
                   rq_ref, rk_ref, rv_ref, rg_ref, cq_ref, ckv_ref, kr_ref, *, nb, rb):
    m = nb * rb
    h = x_ref[...] * (1.0 + sc_ref[...]) + sh_ref[...]
    hb = h.astype(BF16).reshape(m, D_MODEL)
    cos = _tile_rows(cos_ref[...], nb)
    sin = _tile_rows(sin_ref[...], nb)
    lane = lax.broadcasted_iota(jnp.int32, (m, LANES), 1)
    first_half = (lane % 64) < 32

    def proj(lo, hi):
        return _dot(hb, w_ref[:, lo:hi])

    def rmsnorm(y, g):
        ms = jnp.mean(y * y, axis=-1, keepdims=True)
        return y * lax.rsqrt(ms + EPS) * g

    o = 0
    rq = _rope_cols(proj(o, o + RET_QK), cos, sin, first_half) * (RET_DK ** -0.5)
    rq_ref[...] = rq.astype(BF16).reshape(nb, rb, RET_QK)
    o += RET_QK
    rk = _rope_cols(proj(o, o + RET_QK), cos, sin, first_half)
    rk_ref[...] = rk.astype(BF16).reshape(nb, rb, RET_QK)
    o += RET_QK
    rv_ref[...] = proj(o, o + RET_WIDTH).astype(BF16).reshape(nb, rb, RET_WIDTH)
    o += RET_WIDTH
    rg_ref[...] = proj(o, o + RET_WIDTH).astype(BF16).reshape(nb, rb, RET_WIDTH)
    o += RET_WIDTH
    cq_ref[...] = rmsnorm(proj(o, o + Q_LORA), gcq_ref[...]).astype(BF16).reshape(nb, rb, Q_LORA)
    o += Q_LORA
    ckv_ref[...] = rmsnorm(proj(o, o + KV_LORA), gckv_ref[...]).reshape(nb, rb, KV_LORA)
    o += KV_LORA
    kr = _rope_cols(proj(o, o + LANES), cos, sin, first_half)
    kr_ref[...] = kr[:, :MLA_ROPE].reshape(nb, rb, MLA_ROPE)


def _inproj(x, sc, sh, w_in_b, g_cq, g_ckv, cos, sin):
    batch, seq, _ = x.shape
    nb, rb = _row_tiling(batch, seq, ROW_TILE)
    grid = (batch // nb, seq // rb)
    row = lambda w: pl.BlockSpec((nb, rb, w), lambda b, i: (b, i, 0))
    mod = pl.BlockSpec((nb, 1, D_MODEL), lambda b, i: (b, 0, 0))
    tab = pl.BlockSpec((rb, LANES), lambda b, i: (i, 0))
    sds = lambda w, dt: jax.ShapeDtypeStruct((batch, seq, w), dt)
    return pl.pallas_call(
        functools.partial(_inproj_kernel, nb=nb, rb=rb),
        grid=grid,
        in_specs=[row(D_MODEL), mod, mod, _const_spec((D_MODEL, IN_W)),
                  _const_spec((1, Q_LORA)), _const_spec((1, KV_LORA)), tab, tab],
        out_specs=[row(RET_QK), row(RET_QK), row(RET_WIDTH), row(RET_WIDTH),
                   row(Q_LORA), row(KV_LORA), row(MLA_ROPE)],
        out_shape=[sds(RET_QK, BF16), sds(RET_QK, BF16), sds(RET_WIDTH, BF16), sds(RET_WIDTH, BF16),
                   sds(Q_LORA, BF16), sds(KV_LORA, F32), sds(MLA_ROPE, F32)],
        compiler_params=_params(("parallel", "parallel")),
        name="in_proj",
    )(x, sc, sh, w_in_b, g_cq, g_ckv, cos, sin)


def _ret_tables(c):
    hh = jnp.arange(RET_HEADS, dtype=F32)
    logg = jnp.log1p(-jnp.exp2(-5.0 - hh))
    i = jnp.arange(c, dtype=F32)
    diff = i[:, None] - i[None, :]
    dmask = jnp.where(diff[None] >= 0.0, jnp.exp(jnp.maximum(diff, 0.0)[None] * logg[:, None, None]), 0.0)
    qdec = jnp.exp((i + 1.0)[:, None] * logg[None, :])
    kdec = jnp.exp((c - 1.0 - i)[:, None] * logg[None, :])
    cdec = jnp.exp(c * logg)
    qdec = jnp.repeat(qdec, RET_DK, axis=1)
    kdec = jnp.repeat(kdec, RET_DK, axis=1)
    r = jnp.arange(2 * RET_DK)[:, None] // RET_DK
    col = jnp.arange(2 * RET_DV)[None, :] // RET_DV
    pair = jnp.arange(RET_HEADS // 2)[:, None, None]
    sdec = jnp.where(r[None] == col[None], cdec[2 * pair + r[None]], 0.0)
    return dmask.astype(F32), qdec.astype(F32), kdec.astype(F32), sdec.astype(F32)


def _ret_kernel(rq_ref, rk_ref, rv_ref, rg_ref, s0_ref, dmask_ref, qdec_ref, kdec_ref, sdec_ref,
                gret_ref, bret_ref, o_ref, so_ref, s_scr, *, c):
    n = pl.program_id(1)
    npair = RET_HEADS // 2

    @pl.when(n == 0)
    def _():
        s_scr[...] = jnp.zeros(s_scr.shape, F32)
        for p in range(npair):
            s_scr[p, 0:RET_DK, 0:RET_DV] = s0_ref[0, 2 * p]
            s_scr[p, RET_DK:2 * RET_DK, RET_DV:2 * RET_DV] = s0_ref[0, 2 * p + 1]

    q = rq_ref[0]
    k = rk_ref[0]
    qd = (q.astype(F32) * qdec_ref[...]).astype(BF16)
    kd = (k.astype(F32) * kdec_ref[...]).astype(BF16)
    lane = lax.broadcasted_iota(jnp.int32, (c, LANES), 1)
    lo = lane < RET_DK
    zero = jnp.zeros((c, LANES), BF16)
    for p in range(npair):
        qp = q[:, p * LANES:(p + 1) * LANES]
        kp = k[:, p * LANES:(p + 1) * LANES]
        vp = rv_ref[0, :, p * 2 * RET_DV:(p + 1) * 2 * RET_DV]
        sc0 = _dot_nt(jnp.where(lo, qp, zero), kp) * dmask_ref[2 * p]
        sc1 = _dot_nt(jnp.where(lo, zero, qp), kp) * dmask_ref[2 * p + 1]
        intra0 = _dot(sc0.astype(BF16), vp[:, :RET_DV])
        intra1 = _dot(sc1.astype(BF16), vp[:, RET_DV:])
        s = s_scr[p]
        inter = _dot(qd[:, p * LANES:(p + 1) * LANES], s.astype(BF16))
        kv = _dot_tn(kd[:, p * LANES:(p + 1) * LANES], vp)
        sdec = sdec_ref[p]
        s_scr[p] = s * sdec + jnp.where(sdec > 0.0, kv, 0.0)
        for hh in range(2):
            hd = 2 * p + hh
            cols = slice(hd * RET_DV, (hd + 1) * RET_DV)
            oh = (intra0 if hh == 0 else intra1) + inter[:, hh * RET_DV:(hh + 1) * RET_DV]
            mu = jnp.mean(oh, axis=-1, keepdims=True)
            d = oh - mu
            var = jnp.mean(d * d, axis=-1, keepdims=True)
            y = d * lax.rsqrt(var + EPS) * gret_ref[:, cols] + bret_ref[:, cols]
            gate = rg_ref[0, :, cols].astype(F32)
            o_ref[0, :, cols] = (_silu(gate) * y).astype(BF16)

    @pl.when(n == pl.num_programs(1) - 1)
    def _():
        for p in range(npair):
            so_ref[0, 2 * p] = s_scr[p, 0:RET_DK, 0:RET_DV]
            so_ref[0, 2 * p + 1] = s_scr[p, RET_DK:2 * RET_DK, RET_DV:2 * RET_DV]


def _retention(rq, rk, rv, rg, s0, g_ret, b_ret):
    batch, seq, _ = rq.shape
    c = min(RET_CHUNK, seq)
    assert seq % c == 0
    dmask, qdec, kdec, sdec = _ret_tables(c)
    row = lambda w: pl.BlockSpec((1, c, w), lambda b, n: (b, n, 0))
    st = pl.BlockSpec((1, RET_HEADS, RET_DK, RET_DV), lambda b, n: (b, 0, 0, 0))
    return pl.pallas_call(
        functools.partial(_ret_kernel, c=c),
        grid=(batch, seq // c),
        in_specs=[row(RET_QK), row(RET_QK), row(RET_WIDTH), row(RET_WIDTH), st,
                  _const_spec(dmask.shape), _const_spec(qdec.shape), _const_spec(kdec.shape),
                  _const_spec(sdec.shape), _const_spec((1, RET_WIDTH)), _const_spec((1, RET_WIDTH))],
        out_specs=[row(RET_WIDTH), st],
        out_shape=[jax.ShapeDtypeStruct((batch, seq, RET_WIDTH), BF16),
                   jax.ShapeDtypeStruct((batch, RET_HEADS, RET_DK, RET_DV), F32)],
        scratch_shapes=[pltpu.VMEM((RET_HEADS // 2, 2 * RET_DK, 2 * RET_DV), F32)],
        compiler_params=_params(("parallel", "arbitrary")),
        name="retention",
    )(rq, rk, rv, rg, s0, dmask, qdec, kdec, sdec, g_ret, b_ret)


def _upproj_kernel(cq_ref, ckv_ref, kr_ref, cos_ref, sin_ref, wqt_ref, wk_ref, wvt_ref,
                   qt_ref, k_ref, vt_ref):
    cq = cq_ref[0]
    ckv = ckv_ref[0].astype(BF16)
    krb = kr_ref[0].astype(BF16)
    cos = cos_ref[...]
    sin = sin_ref[...]
    half = MLA_ROPE // 2
    qt = _dot_nt(wqt_ref[...], cq) * (MLA_SCALE * LOG2E)
    kn = _dot(ckv, wk_ref[...]).astype(BF16)
    vt = _dot_nt(wvt_ref[...], ckv).astype(BF16)
    for hd in range(MLA_HEADS):
        r0 = hd * MLA_QK
        x1 = qt[r0 + MLA_NOPE:r0 + MLA_NOPE + half]
        x2 = qt[r0 + MLA_NOPE + half:r0 + MLA_QK]
        qt_ref[0, hd, 0:MLA_NOPE] = qt[r0:r0 + MLA_NOPE].astype(BF16)
        qt_ref[0, hd, MLA_NOPE:MLA_NOPE + half] = (x1 * cos - x2 * sin).astype(BF16)
        qt_ref[0, hd, MLA_NOPE + half:MLA_QK] = (x2 * cos + x1 * sin).astype(BF16)
        k_ref[0, hd, :, 0:MLA_NOPE] = kn[:, hd * MLA_NOPE:(hd + 1) * MLA_NOPE]
        k_ref[0, hd, :, MLA_NOPE:MLA_QK] = krb
        vt_ref[0, hd] = vt[hd * MLA_DV:(hd + 1) * MLA_DV]


def _upproj(cq, ckv, kr, cos_t, sin_t, wqt, wk, wvt):
    batch, seq, _ = cq.shape
    rb = min(ROW_TILE, seq)
    assert seq % rb == 0
    row = lambda w: pl.BlockSpec((1, rb, w), lambda b, i: (b, i, 0))
    tab = pl.BlockSpec((MLA_ROPE // 2, rb), lambda b, i: (0, i))
    head_t = lambda w: pl.BlockSpec((1, MLA_HEADS, w, rb), lambda b, i: (b, 0, 0, i))
    return pl.pallas_call(
        _upproj_kernel,
        grid=(batch, seq // rb),
        in_specs=[row(Q_LORA), row(KV_LORA), row(MLA_ROPE), tab, tab,
                  _const_spec(wqt.shape), _const_spec(wk.shape), _const_spec(wvt.shape)],
        out_specs=[head_t(MLA_QK),
                   pl.BlockSpec((1, MLA_HEADS, rb, MLA_QK), lambda b, i: (b, 0, i, 0)),
                   head_t(MLA_DV)],
        out_shape=[jax.ShapeDtypeStruct((batch, MLA_HEADS, MLA_QK, seq), BF16),
                   jax.ShapeDtypeStruct((batch, MLA_HEADS, seq, MLA_QK), BF16),
                   jax.ShapeDtypeStruct((batch, MLA_HEADS, MLA_DV, seq), BF16)],
        compiler_params=_params(("parallel", "parallel")),
        name="mla_up_proj",
    )(cq, ckv, kr, cos_t, sin_t, wqt, wk, wvt)


def _flash_kernel(qt_ref, k_ref, vt_ref, o_ref, m_scr, l_scr, acc_scr, sa_scr, sb_scr, *, t):
    i = pl.program_id(2)
    m_scr[...] = jnp.full(m_scr.shape, -jnp.inf, F32)
    l_scr[...] = jnp.zeros(l_scr.shape, F32)
    acc_scr[...] = jnp.zeros(acc_scr.shape, F32)

    heads = range(ATTN_HEADS_PER_STEP)

    def scores(j, hd):
        start = pl.multiple_of(j * t, t)
        return _dot(k_ref[0, hd, pl.ds(start, t), :], qt_ref[0, hd])

    def update(s, j, hd, masked):
        start = pl.multiple_of(j * t, t)
        vtb = vt_ref[0, hd, :, pl.ds(start, t)]
        if masked:
            kc = lax.broadcasted_iota(jnp.int32, (t, t), 0) // CHUNK
            qc = lax.broadcasted_iota(jnp.int32, (t, t), 1) // CHUNK
            s = jnp.where(kc <= qc, s, NEG_BIG)
        m_prev = m_scr[hd]
        m_new = jnp.maximum(m_prev, jnp.max(s, axis=0, keepdims=True))
        alpha = jnp.exp2(m_prev - m_new)
        p = jnp.exp2(s - m_new)
        l_scr[hd] = alpha * l_scr[hd] + jnp.sum(p, axis=0, keepdims=True)
        acc_scr[hd] = alpha * acc_scr[hd] + _dot(vtb, p.astype(BF16))
        m_scr[hd] = m_new

    def pipelined(s_cur, s_nxt, j):
        for hd in heads:
            s_nxt[hd] = scores(j + 1, hd)
            update(s_cur[hd], j, hd, False)

    def diagonal(s_cur):
        for hd in heads:
            update(s_cur[hd], i, hd, True)

    for hd in heads:
        sa_scr[hd] = scores(0, hd)

    def body(jj, carry):
        pipelined(sa_scr, sb_scr, 2 * jj)
        pipelined(sb_scr, sa_scr, 2 * jj + 1)
        return carry

    lax.fori_loop(0, lax.shift_right_logical(i, 1), body, 0)

    @pl.when((i & 1) == 1)
    def _():
        pipelined(sa_scr, sb_scr, i - 1)
        diagonal(sb_scr)

    @pl.when((i & 1) == 0)
    def _():
        diagonal(sa_scr)

    for hd in heads:
        o_ref[0, :, hd * MLA_DV:(hd + 1) * MLA_DV] = jnp.transpose(acc_scr[hd] / l_scr[hd]).astype(BF16)


def _flash(qt, k, vt):
    batch, heads, seq, _ = k.shape
    t = min(ATTN_TILE, seq)
    g = ATTN_HEADS_PER_STEP
    assert seq % t == 0 and t % CHUNK == 0 and heads % g == 0
    return pl.pallas_call(
        functools.partial(_flash_kernel, t=t),
        grid=(batch, heads // g, seq // t),
        in_specs=[pl.BlockSpec((1, g, MLA_QK, t), lambda b, h, i: (b, h, 0, i)),
                  pl.BlockSpec((1, g, seq, MLA_QK), lambda b, h, i: (b, h, 0, 0)),
                  pl.BlockSpec((1, g, MLA_DV, seq), lambda b, h, i: (b, h, 0, 0))],
        out_specs=pl.BlockSpec((1, t, g * MLA_DV), lambda b, h, i: (b, i, h)),
        out_shape=jax.ShapeDtypeStruct((batch, seq, MLA_WIDTH), BF16),
        scratch_shapes=[pltpu.VMEM((g, 1, t), F32), pltpu.VMEM((g, 1, t), F32),
                        pltpu.VMEM((g, MLA_DV, t), F32), pltpu.VMEM((g, t, t), F32), pltpu.VMEM((g, t, t), F32)],
        compiler_params=_params(("parallel", "parallel", "arbitrary")),
        name="mla_flash",
    )(qt, k, vt)


def _cached_attn_kernel(cq_ref, ckvn_ref, krn_ref, ckvp_ref, krp_ref, cos_ref, sin_ref,
                        wqn_ref, wqr_ref, wkt_ref, wv_ref, o_ref, *, tq, past):
    cq = cq_ref[0]
    lane = lax.broadcasted_iota(jnp.int32, (tq, LANES), 1)
    first_half = (lane % 64) < 32
    qscale = MLA_SCALE * LOG2E
    qn = (_dot(cq, wqn_ref[...]) * qscale).astype(BF16)
    qr = (_rope_cols(_dot(cq, wqr_ref[...]), cos_ref[...], sin_ref[...], first_half) * qscale).astype(BF16)
    qlat = jnp.concatenate(
        [_dot(qn[:, hd * MLA_NOPE:(hd + 1) * MLA_NOPE], wkt_ref[hd]) for hd in range(MLA_HEADS)],
        axis=0).astype(BF16)
    qrr = jnp.concatenate([qr[:, hd * MLA_ROPE:(hd + 1) * MLA_ROPE] for hd in range(MLA_HEADS)], axis=0)
    ckv_p = ckvp_ref[0].astype(BF16)
    kr_p = krp_ref[0].astype(BF16)
    ckv_n = ckvn_ref[0].astype(BF16)
    kr_n = krn_ref[0].astype(BF16)
    s_p = _dot_nt(qlat, ckv_p) + _dot_nt(qrr, kr_p)
    s_n = _dot_nt(qlat, ckv_n) + _dot_nt(qrr, kr_n)
    rows = MLA_HEADS * tq
    qpos = past + lax.broadcasted_iota(jnp.int32, (rows, tq), 0) % tq
    kpos = past + lax.broadcasted_iota(jnp.int32, (rows, tq), 1)
    s_n = jnp.where(kpos // CHUNK <= qpos // CHUNK, s_n, NEG_BIG)
    m = jnp.maximum(jnp.max(s_p, axis=1, keepdims=True), jnp.max(s_n, axis=1, keepdims=True))
    p_p = jnp.exp2(s_p - m)
    p_n = jnp.exp2(s_n - m)
    l = jnp.sum(p_p, axis=1, keepdims=True) + jnp.sum(p_n, axis=1, keepdims=True)
    olat = (_dot(p_p.astype(BF16), ckv_p) + _dot(p_n.astype(BF16), ckv_n)) / l
    olat = olat.astype(BF16)
    for hd in range(MLA_HEADS):
        o_ref[0, :, hd * MLA_DV:(hd + 1) * MLA_DV] = _dot(olat[hd * tq:(hd + 1) * tq], wv_ref[hd]).astype(BF16)


def _cached_attn(cq, ckv_new, kr_new, ckv_past, kr_past, cos, sin, wqn, wqr, wkt, wvh):
    batch, tq, _ = cq.shape
    past = ckv_past.shape[1]
    assert past % CHUNK == 0
    one = lambda n, w: pl.BlockSpec((1, n, w), lambda b: (b, 0, 0))
    return pl.pallas_call(
        functools.partial(_cached_attn_kernel, tq=tq, past=past),
        grid=(batch,),
        in_specs=[one(tq, Q_LORA), one(tq, KV_LORA), one(tq, MLA_ROPE), one(past, KV_LORA), one(past, MLA_ROPE),
                  _const_spec(cos.shape), _const_spec(sin.shape), _const_spec(wqn.shape), _const_spec(wqr.shape),
                  _const_spec(wkt.shape), _const_spec(wvh.shape)],
        out_specs=one(tq, MLA_WIDTH),
        out_shape=jax.ShapeDtypeStruct((batch, tq, MLA_WIDTH), BF16),
        compiler_params=_params(("parallel",)),
        name="mla_cached",
    )(cq, ckv_new, kr_new, ckv_past, kr_past, cos, sin, wqn, wqr, wkt, wvh)


def _layernorm(y, g, b):
    mu = jnp.mean(y, axis=-1, keepdims=True)
    d = y - mu
    var = jnp.mean(d * d, axis=-1, keepdims=True)
    return d * lax.rsqrt(var + EPS) * g + b


def _outproj_kernel(oret_ref, omla_ref, x_ref, g1_ref, w_ref, lng_ref, lnb_ref, x1_ref, *, nb, rb):
    m = nb * rb
    a = oret_ref[...].reshape(m, RET_WIDTH)
    b = omla_ref[...].reshape(m, MLA_WIDTH)
    mix = _dot(a, w_ref[0:RET_WIDTH, :]) + _dot(b, w_ref[RET_WIDTH:RET_WIDTH + MLA_WIDTH, :])
    y = ALPHA * x_ref[...] + g1_ref[...] * mix.reshape(nb, rb, D_MODEL)
    x1_ref[...] = _layernorm(y, lng_ref[...], lnb_ref[...])


def _outproj(o_ret, o_mla, x, g1, w_out_b, ln_g, ln_b):
    batch, seq, _ = x.shape
    nb, rb = _row_tiling(batch, seq, ROW_TILE)
    row = lambda w: pl.BlockSpec((nb, rb, w), lambda b, i: (b, i, 0))
    mod = pl.BlockSpec((nb, 1, D_MODEL), lambda b, i: (b, 0, 0))
    return pl.pallas_call(
        functools.partial(_outproj_kernel, nb=nb, rb=rb),
        grid=(batch // nb, seq // rb),
        in_specs=[row(RET_WIDTH), row(MLA_WIDTH), row(D_MODEL), mod, _const_spec(w_out_b.shape),
                  _const_spec((1, 1, D_MODEL)), _const_spec((1, 1, D_MODEL))],
        out_specs=row(D_MODEL),
        out_shape=jax.ShapeDtypeStruct((batch, seq, D_MODEL), F32),
        compiler_params=_params(("parallel", "parallel")),
        name="out_proj_ln1",
    )(o_ret, o_mla, x, g1, w_out_b, ln_g, ln_b)


def _ffn_kernel(x1_ref, sc_ref, sh_ref, g2_ref, wg_ref, wu_ref, wd_ref, lng_ref, lnb_ref, o_ref, h_scr,
                *, nb, rb):
    m = nb * rb
    j = pl.program_id(2)

    @pl.when(j == 0)
    def _():
        x1 = x1_ref[...]
        h_scr[...] = (x1 * (1.0 + sc_ref[...]) + sh_ref[...]).astype(BF16).reshape(m, D_MODEL)
        o_ref[...] = ALPHA * x1

    h = h_scr[...]
    act = (_silu(_dot(h, wg_ref[...])) * _dot(h, wu_ref[...])).astype(BF16)
    for n in range(D_MODEL // FFN_COL_TILE):
        cols = slice(n * FFN_COL_TILE, (n + 1) * FFN_COL_TILE)
        part = _dot(act, wd_ref[:, cols]).reshape(nb, rb, FFN_COL_TILE)
        o_ref[:, :, cols] += part * g2_ref[:, :, cols]

    @pl.when(j == pl.num_programs(2) - 1)
    def _():
        o_ref[...] = _layernorm(o_ref[...], lng_ref[...], lnb_ref[...])


def _ffn(x1, sc, sh, g2, wg, wu, wd, ln_g, ln_b):
    batch, seq, _ = x1.shape
    nb, rb = _row_tiling(batch, seq, FFN_ROW_TILE)
    tf = FFN_COL_TILE
    assert D_FF % tf == 0
    row = pl.BlockSpec((nb, rb, D_MODEL), lambda b, i, j: (b, i, 0))
    mod = pl.BlockSpec((nb, 1, D_MODEL), lambda b, i, j: (b, 0, 0))
    return pl.pallas_call(
        functools.partial(_ffn_kernel, nb=nb, rb=rb),
        grid=(batch // nb, seq // rb, D_FF // tf),
        in_specs=[pl.BlockSpec((nb, rb, D_MODEL), lambda b, i, j: (b, i, 0), pipeline_mode=pl.Buffered(1)),
                  mod, mod, mod,
                  pl.BlockSpec((D_MODEL, tf), lambda b, i, j: (0, j)),
                  pl.BlockSpec((D_MODEL, tf), lambda b, i, j: (0, j)),
                  pl.BlockSpec((tf, D_MODEL), lambda b, i, j: (j, 0)),
                  _const_spec((1, 1, D_MODEL)), _const_spec((1, 1, D_MODEL))],
        out_specs=row,
        out_shape=jax.ShapeDtypeStruct((batch, seq, D_MODEL), F32),
        scratch_shapes=[pltpu.VMEM((nb * rb, D_MODEL), BF16)],
        compiler_params=_params(("parallel", "parallel", "arbitrary")),
        name="swiglu_ln2",
    )(x1, sc, sh, g2, wg, wu, wd, ln_g, ln_b)


def _layer(x, mods, pos, ckv_past, kr_past, ret_s0, wts):
    sh1, sc1, g1, sh2, sc2, g2 = mods
    cos, sin = _rope_tables(pos)
    rq, rk, rv, rg, cq, ckv, kr = _inproj(x, sc1, sh1, wts["w_in"], wts["g_cq"], wts["g_ckv"], cos, sin)
    o_ret, ret_s1 = _retention(rq, rk, rv, rg, ret_s0, wts["g_ret"], wts["b_ret"])
    if ckv_past is None:
        cos_t, sin_t = _rope_tables_t(pos)
        qt, k, vt = _upproj(cq, ckv, kr, cos_t, sin_t, wts["wqt"], wts["wk"], wts["wvt"])
        o_mla = _flash(qt, k, vt)
    else:
        o_mla = _cached_attn(cq, ckv, kr, ckv_past, kr_past, cos, sin,
                             wts["wqn"], wts["wqr"], wts["wkt"], wts["wvh"])
    x1 = _outproj(o_ret, o_mla, x, g1, wts["w_out"], wts["ln1_g"], wts["ln1_b"])
    y = _ffn(x1, sc2, sh2, g2, wts["w_gate"], wts["w_up"], wts["w_down"], wts["ln2_g"], wts["ln2_b"])
    return y, ckv, kr, ret_s1


def _prep_weights(l, w_in, g_cq, g_ckv, w_uq, w_uk, w_uv, g_ret, b_ret, w_out, ln1_g, ln1_b,
                  w_gate, w_up, w_down, ln2_g, ln2_b):
    kr_lo = 2 * RET_QK + 2 * RET_WIDTH + Q_LORA + KV_LORA
    w_in_l = w_in[l]
    return {
        "w_in": jnp.concatenate([w_in_l, w_in_l[:, kr_lo:kr_lo + MLA_ROPE]], axis=1).astype(BF16),
        "g_cq": g_cq[l].reshape(1, Q_LORA),
        "g_ckv": g_ckv[l].reshape(1, KV_LORA),
        "wqn": w_uq[l][:, :, :MLA_NOPE].reshape(Q_LORA, MLA_HEADS * MLA_NOPE).astype(BF16),
        "wqr": w_uq[l][:, :, MLA_NOPE:].reshape(Q_LORA, MLA_HEADS * MLA_ROPE).astype(BF16),
        "wk": w_uk[l].reshape(KV_LORA, MLA_HEADS * MLA_NOPE).astype(BF16),
        "wqt": jnp.transpose(w_uq[l].reshape(Q_LORA, MLA_HEADS * MLA_QK)).astype(BF16),
        "wvt": jnp.transpose(w_uv[l].reshape(KV_LORA, MLA_HEADS * MLA_DV)).astype(BF16),
        "wkt": jnp.transpose(w_uk[l], (1, 2, 0)).astype(BF16),
        "wvh": jnp.transpose(w_uv[l], (1, 0, 2)).astype(BF16),
        "g_ret": g_ret[l].reshape(1, RET_WIDTH),
        "b_ret": b_ret[l].reshape(1, RET_WIDTH),
        "w_out": w_out[l].astype(BF16),
        "ln1_g": ln1_g[l].reshape(1, 1, D_MODEL),
        "ln1_b": ln1_b[l].reshape(1, 1, D_MODEL),
        "w_gate": w_gate[l].astype(BF16),
        "w_up": w_up[l].astype(BF16),
        "w_down": w_down[l].astype(BF16),
        "ln2_g": ln2_g[l].reshape(1, 1, D_MODEL),
        "ln2_b": ln2_b[l].reshape(1, 1, D_MODEL),
    }


def kernel(x_prompt, x_sample, c_prompt, c_sample, cache_mla_ckv, cache_mla_krope, state_ret, w_ada, b_ada, w_in, g_cq, g_ckv, w_uq, w_uk, w_uv, g_ret, b_ret, w_out, ln1_g, ln1_b, w_gate, w_up, w_down, ln2_g, ln2_b):
    bp, tp, _ = x_prompt.shape
    bs, ts, _ = x_sample.shape
    past = cache_mla_ckv.shape[2]
    pos_p = jnp.arange(tp, dtype=jnp.int32)
    pos_s = past + jnp.arange(ts, dtype=jnp.int32)
    c_all = jnp.concatenate([c_prompt, c_sample], axis=0)
    yp, ys = x_prompt, x_sample
    outs = [[] for _ in range(6)]
    for l in range(w_ada.shape[0]):
        wts = _prep_weights(l, w_in, g_cq, g_ckv, w_uq, w_uk, w_uv, g_ret, b_ret, w_out, ln1_g, ln1_b,
                            w_gate, w_up, w_down, ln2_g, ln2_b)
        mod = _ada(c_all, w_ada[l], b_ada[l]).reshape(bp + bs, 6, 1, D_MODEL)
        mods_p = tuple(mod[:bp, i] for i in range(6))
        mods_s = tuple(mod[bp:, i] for i in range(6))
        s0 = jnp.zeros((bp, RET_HEADS, RET_DK, RET_DV), F32)
        yp, ckv_p, kr_p, rs_p = _layer(yp, mods_p, pos_p, None, None, s0, wts)
        ys, ckv_s, kr_s, rs_s = _layer(ys, mods_s, pos_s, cache_mla_ckv[l], cache_mla_krope[l],
                                       state_ret[l], wts)
        for lst, val in zip(outs, (ckv_p, kr_p, rs_p, ckv_s, kr_s, rs_s)):
            lst.append(val)
    stacked = [jnp.stack(v, axis=0) for v in outs]
    return (yp, ys, *stacked)
```

```python
import functools
import math

import jax
import jax.numpy as jnp
import numpy as np
from jax import lax
from jax.experimental import pallas as pl
from jax.experimental.pallas import tpu as pltpu

F32 = jnp.float32
BF16 = jnp.bfloat16

D_MODEL = 2048
CHUNK = 64
RET_HEADS = 8
RET_DK = 64
RET_DV = 128
RET_QK = RET_HEADS * RET_DK
RET_WIDTH = RET_HEADS * RET_DV
MLA_HEADS = 8
MLA_NOPE = 128
MLA_ROPE = 64
MLA_DV = 128
MLA_DV_AUG = MLA_DV + 16
MLA_QK = MLA_NOPE + MLA_ROPE
MLA_WIDTH = MLA_HEADS * MLA_DV
Q_LORA = 512
KV_LORA = 512
MLA_SCALE = MLA_QK ** -0.5
D_FF = 5632
DEPTH = 1
ALPHA = (2 * DEPTH) ** 0.25
ROPE_BASE = 10000.0
EPS = 1e-5
LOG2E = math.log2(math.e)
NEG_BIG = -1e30

LANES = 128
ROW_TILE = 512
FFN_ROW_TILE = 1024
FFN_COL_TILE = 512
RET_CHUNK = 256
ATTN_Q_TILE = 1024
ATTN_K_TILE = 512
ATTN_HEADS_PER_STEP = 2
ADA_COL_TILE = 1024
VMEM_LIMIT = 56 * 1024 * 1024
FFN_VMEM_LIMIT = 60 * 1024 * 1024


def _params(sem, vmem=VMEM_LIMIT):
    return pltpu.CompilerParams(dimension_semantics=sem, vmem_limit_bytes=vmem)


def _const_spec(shape):
    zeros = (0,) * len(shape)
    return pl.BlockSpec(shape, lambda *_: zeros, pipeline_mode=pl.Buffered(1))


def _dot(a, b):
    return jnp.dot(a, b, preferred_element_type=F32)


def _dot_nt(a, b):
    return lax.dot_general(a, b, (((1,), (1,)), ((), ())), preferred_element_type=F32)


def _dot_tn(a, b):
    return lax.dot_general(a, b, (((0,), (0,)), ((), ())), preferred_element_type=F32)


def _silu(x):
    return x * jax.nn.sigmoid(x)


def _row_tiling(batch, seq, rows):
    if seq >= rows:
        assert seq % rows == 0
        return 1, rows
    nb = min(batch, rows // seq)
    assert batch % nb == 0
    return nb, seq


def _rope_cols(y, cos, sin, first_half):
    outs = []
    for g in range(y.shape[1] // LANES):
        yg = y[:, g * LANES:(g + 1) * LANES]
        rot = jnp.where(first_half, pltpu.roll(yg, LANES - 32, 1), pltpu.roll(yg, 32, 1))
        outs.append(yg * cos + rot * sin)
    return outs[0] if len(outs) == 1 else jnp.concatenate(outs, axis=1)


def _rope_angles(pos0, n):
    half = 32
    inv = ROPE_BASE ** (-np.arange(half, dtype=np.float64) / half)
    pos = (pos0 + np.arange(n)).astype(np.float64)
    return pos[:, None] * inv[None, :]


def _rope_tables(pos0, n):
    ang = _rope_angles(pos0, n)
    cos = np.cos(ang).astype(np.float32)
    sin = np.sin(ang).astype(np.float32)
    return np.concatenate([cos, cos, cos, cos], axis=1), np.concatenate([-sin, sin, -sin, sin], axis=1)


def _rope_tables_t(pos0, n):
    ang = _rope_angles(pos0, n)
    return (np.ascontiguousarray(np.cos(ang).T.astype(np.float32)),
            np.ascontiguousarray(np.sin(ang).T.astype(np.float32)))


def _tile_rows(t, nb):
    if nb == 1:
        return t
    return jnp.broadcast_to(t[None], (nb,) + t.shape).reshape(nb * t.shape[0], t.shape[1])


def _ada_kernel(c_ref, w_ref, b_ref, o_ref):
    c = c_ref[...]
    a = _silu(c).astype(BF16)
    o_ref[...] = _dot(a, w_ref[...].astype(BF16)) + b_ref[...]


def _ada(c, w_ada, b_ada):
    nrow = c.shape[0]
    ncol = w_ada.shape[1]
    tn = ADA_COL_TILE
    return pl.pallas_call(
        _ada_kernel,
        grid=(ncol // tn,),
        in_specs=[_const_spec((nrow, D_MODEL)),
                  pl.BlockSpec((D_MODEL, tn), lambda j: (0, j)),
                  pl.BlockSpec((1, tn), lambda j: (0, j))],
        out_specs=pl.BlockSpec((nrow, tn), lambda j: (0, j)),
        out_shape=jax.ShapeDtypeStruct((nrow, ncol), F32),
        compiler_params=_params(("arbitrary",)),
        name="ada_mod",
    )(c, w_ada, b_ada.reshape(1, ncol))


IN_W = 2 * RET_QK + 2 * RET_WIDTH + Q_LORA + KV_LORA + 2 * MLA_ROPE


def _inproj_kernel(x_ref, sc_ref, sh_ref, w_ref, gcq_ref, gckv_ref, cos_ref, sin_ref,
                   rq_ref, rk_ref, rv_ref, rg_ref, cq_ref, ckv_ref, kr_ref, *, nb, rb):
    m = nb * rb
    h = x_ref[...] * (1.0 + sc_ref[...]) + sh_ref[...]
    hb = h.astype(BF16).reshape(m, D_MODEL)
    cos = _tile_rows(cos_ref[...], nb)
    sin = _tile_rows(sin_ref[...], nb)
    lane = lax.broadcasted_iota(jnp.int32, (m, LANES), 1)
    first_half = (lane % 64) < 32

    def proj(lo, hi):
        return _dot(hb, w_ref[:, lo:hi])

    def rmsnorm(y, g):
        ms = jnp.mean(y * y, axis=-1, keepdims=True)
        return y * lax.rsqrt(ms + EPS) * g

    o = 0
    rq = _rope_cols(proj(o, o + RET_QK), cos, sin, first_half) * (RET_DK ** -0.5)
    rq_ref[...] = rq.astype(BF16).reshape(nb, rb, RET_QK)
    o += RET_QK
    rk = _rope_cols(proj(o, o + RET_QK), cos, sin, first_half)
    rk_ref[...] = rk.astype(BF16).reshape(nb, rb, RET_QK)
    o += RET_QK
    rv_ref[...] = proj(o, o + RET_WIDTH).astype(BF16).reshape(nb, rb, RET_WIDTH)
    o += RET_WIDTH
    rg_ref[...] = proj(o, o + RET_WIDTH).astype(BF16).reshape(nb, rb, RET_WIDTH)
    o += RET_WIDTH
    cq_ref[...] = rmsnorm(proj(o, o + Q_LORA), gcq_ref[...]).astype(BF16).reshape(nb, rb, Q_LORA)
    o += Q_LORA
    ckv_ref[...] = rmsnorm(proj(o, o + KV_LORA), gckv_ref[...]).reshape(nb, rb, KV_LORA)
    o += KV_LORA
    kr = _rope_cols(proj(o, o + LANES), cos, sin, first_half)
    kr_ref[...] = kr[:, :MLA_ROPE].reshape(nb, rb, MLA_ROPE)


def _inproj(x, sc, sh, w_in_b, g_cq, g_ckv, cos, sin):
    batch, seq, _ = x.shape
    nb, rb = _row_tiling(batch, seq, ROW_TILE)
    grid = (batch // nb, seq // rb)
    row = lambda w: pl.BlockSpec((nb, rb, w), lambda b, i: (b, i, 0))
    mod = pl.BlockSpec((nb, 1, D_MODEL), lambda b, i: (b, 0, 0))
    tab = pl.BlockSpec((rb, LANES), lambda b, i: (i, 0))
    sds = lambda w, dt: jax.ShapeDtypeStruct((batch, seq, w), dt)
    return pl.pallas_call(
        functools.partial(_inproj_kernel, nb=nb, rb=rb),
        grid=grid,
        in_specs=[row(D_MODEL), mod, mod, _const_spec((D_MODEL, IN_W)),
                  _const_spec((1, Q_LORA)), _const_spec((1, KV_LORA)), tab, tab],
        out_specs=[row(RET_QK), row(RET_QK), row(RET_WIDTH), row(RET_WIDTH),
                   row(Q_LORA), row(KV_LORA), row(MLA_ROPE)],
        out_shape=[sds(RET_QK, BF16), sds(RET_QK, BF16), sds(RET_WIDTH, BF16), sds(RET_WIDTH, BF16),
                   sds(Q_LORA, BF16), sds(KV_LORA, F32), sds(MLA_ROPE, F32)],
        compiler_params=_params(("parallel", "parallel")),
        name="in_proj",
    )(x, sc, sh, w_in_b, g_cq, g_ckv, cos, sin)


def _ret_tables(c):
    hh = np.arange(RET_HEADS, dtype=np.float64)
    logg = np.log1p(-np.exp2(-5.0 - hh))
    i = np.arange(c, dtype=np.float64)
    diff = i[:, None] - i[None, :]
    dmask = np.where(diff[None] >= 0.0, np.exp(np.maximum(diff, 0.0)[None] * logg[:, None, None]), 0.0)
    qdec = np.exp((i + 1.0)[:, None] * logg[None, :])
    kdec = np.exp((c - 1.0 - i)[:, None] * logg[None, :])
    cdec = np.exp(c * logg)
    qdec = np.repeat(qdec, RET_DK, axis=1)
    kdec = np.repeat(kdec, RET_DK, axis=1)
    r = np.arange(2 * RET_DK)[:, None] // RET_DK
    col = np.arange(2 * RET_DV)[None, :] // RET_DV
    pair = np.arange(RET_HEADS // 2)[:, None, None]
    sdec = np.where(r[None] == col[None], cdec[2 * pair + r[None]], 0.0)
    return tuple(a.astype(np.float32) for a in (dmask, qdec, kdec, sdec))


def _ret_kernel(rq_ref, rk_ref, rv_ref, rg_ref, s0_ref, dmask_ref, qdec_ref, kdec_ref, sdec_ref,
                gret_ref, bret_ref, o_ref, so_ref, s_scr, *, c):
    n = pl.program_id(1)
    npair = RET_HEADS // 2

    @pl.when(n == 0)
    def _():
        s_scr[...] = jnp.zeros(s_scr.shape, F32)
        for p in range(npair):
            s_scr[p, 0:RET_DK, 0:RET_DV] = s0_ref[0, 2 * p]
            s_scr[p, RET_DK:2 * RET_DK, RET_DV:2 * RET_DV] = s0_ref[0, 2 * p + 1]

    q = rq_ref[0]
    k = rk_ref[0]
    qd = (q.astype(F32) * qdec_ref[...]).astype(BF16)
    kd = (k.astype(F32) * kdec_ref[...]).astype(BF16)
    lane = lax.broadcasted_iota(jnp.int32, (c, LANES), 1)
    lo = lane < RET_DK
    zero = jnp.zeros((c, LANES), BF16)
    for p in range(npair):
        qp = q[:, p * LANES:(p + 1) * LANES]
        kp = k[:, p * LANES:(p + 1) * LANES]
        vp = rv_ref[0, :, p * 2 * RET_DV:(p + 1) * 2 * RET_DV]
        sc0 = _dot_nt(jnp.where(lo, qp, zero), kp) * dmask_ref[2 * p]
        sc1 = _dot_nt(jnp.where(lo, zero, qp), kp) * dmask_ref[2 * p + 1]
        intra0 = _dot(sc0.astype(BF16), vp[:, :RET_DV])
        intra1 = _dot(sc1.astype(BF16), vp[:, RET_DV:])
        s = s_scr[p]
        inter = _dot(qd[:, p * LANES:(p + 1) * LANES], s.astype(BF16))
        kv = _dot_tn(kd[:, p * LANES:(p + 1) * LANES], vp)
        sdec = sdec_ref[p]
        s_scr[p] = s * sdec + jnp.where(sdec > 0.0, kv, 0.0)
        for hh in range(2):
            hd = 2 * p + hh
            cols = slice(hd * RET_DV, (hd + 1) * RET_DV)
            oh = (intra0 if hh == 0 else intra1) + inter[:, hh * RET_DV:(hh + 1) * RET_DV]
            mu = jnp.mean(oh, axis=-1, keepdims=True)
            d = oh - mu
            var = jnp.mean(d * d, axis=-1, keepdims=True)
            y = d * lax.rsqrt(var + EPS) * gret_ref[:, cols] + bret_ref[:, cols]
            gate = rg_ref[0, :, cols].astype(F32)
            o_ref[0, :, cols] = (_silu(gate) * y).astype(BF16)

    @pl.when(n == pl.num_programs(1) - 1)
    def _():
        for p in range(npair):
            so_ref[0, 2 * p] = s_scr[p, 0:RET_DK, 0:RET_DV]
            so_ref[0, 2 * p + 1] = s_scr[p, RET_DK:2 * RET_DK, RET_DV:2 * RET_DV]


def _retention(rq, rk, rv, rg, s0, g_ret, b_ret):
    batch, seq, _ = rq.shape
    c = min(RET_CHUNK, seq)
    assert seq % c == 0
    dmask, qdec, kdec, sdec = _ret_tables(c)
    row = lambda w: pl.BlockSpec((1, c, w), lambda b, n: (b, n, 0))
    st = pl.BlockSpec((1, RET_HEADS, RET_DK, RET_DV), lambda b, n: (b, 0, 0, 0))
    return pl.pallas_call(
        functools.partial(_ret_kernel, c=c),
        grid=(batch, seq // c),
        in_specs=[row(RET_QK), row(RET_QK), row(RET_WIDTH), row(RET_WIDTH), st,
                  _const_spec(dmask.shape), _const_spec(qdec.shape), _const_spec(kdec.shape),
                  _const_spec(sdec.shape), _const_spec((1, RET_WIDTH)), _const_spec((1, RET_WIDTH))],
        out_specs=[row(RET_WIDTH), st],
        out_shape=[jax.ShapeDtypeStruct((batch, seq, RET_WIDTH), BF16),
                   jax.ShapeDtypeStruct((batch, RET_HEADS, RET_DK, RET_DV), F32)],
        scratch_shapes=[pltpu.VMEM((RET_HEADS // 2, 2 * RET_DK, 2 * RET_DV), F32)],
        compiler_params=_params(("parallel", "arbitrary")),
        name="retention",
    )(rq, rk, rv, rg, s0, dmask, qdec, kdec, sdec, g_ret, b_ret)


def _upproj_kernel(cq_ref, ckv_ref, kr_ref, cos_ref, sin_ref, wqt_ref, wk_ref, wvt_ref,
                   qt_ref, k_ref, vt_ref):
    cq = cq_ref[0]
    ckv = ckv_ref[0].astype(BF16)
    krb = kr_ref[0].astype(BF16)
    cos = cos_ref[...]
    sin = sin_ref[...]
    half = MLA_ROPE // 2
    qt = _dot_nt(wqt_ref[...], cq) * (MLA_SCALE * LOG2E)
    kn = _dot(ckv, wk_ref[...]).astype(BF16)
    vt = _dot_nt(wvt_ref[...], ckv).astype(BF16)
    for hd in range(MLA_HEADS):
        r0 = hd * MLA_QK
        x1 = qt[r0 + MLA_NOPE:r0 + MLA_NOPE + half]
        x2 = qt[r0 + MLA_NOPE + half:r0 + MLA_QK]
        qt_ref[0, hd, 0:MLA_NOPE] = qt[r0:r0 + MLA_NOPE].astype(BF16)
        qt_ref[0, hd, MLA_NOPE:MLA_NOPE + half] = (x1 * cos - x2 * sin).astype(BF16)
        qt_ref[0, hd, MLA_NOPE + half:MLA_QK] = (x2 * cos + x1 * sin).astype(BF16)
        k_ref[0, hd, :, 0:MLA_NOPE] = kn[:, hd * MLA_NOPE:(hd + 1) * MLA_NOPE]
        k_ref[0, hd, :, MLA_NOPE:MLA_QK] = krb
        vt_ref[0, hd, 0:MLA_DV] = vt[hd * MLA_DV:(hd + 1) * MLA_DV]
        vt_ref[0, hd, MLA_DV:MLA_DV_AUG] = jnp.ones((MLA_DV_AUG - MLA_DV, vt.shape[1]), BF16)


def _upproj(cq, ckv, kr, cos_t, sin_t, wqt, wk, wvt):
    batch, seq, _ = cq.shape
    rb = min(ROW_TILE, seq)
    assert seq % rb == 0
    row = lambda w: pl.BlockSpec((1, rb, w), lambda b, i: (b, i, 0))
    tab = pl.BlockSpec((MLA_ROPE // 2, rb), lambda b, i: (0, i))
    head_t = lambda w: pl.BlockSpec((1, MLA_HEADS, w, rb), lambda b, i: (b, 0, 0, i))
    return pl.pallas_call(
        _upproj_kernel,
        grid=(batch, seq // rb),
        in_specs=[row(Q_LORA), row(KV_LORA), row(MLA_ROPE), tab, tab,
                  _const_spec(wqt.shape), _const_spec(wk.shape), _const_spec(wvt.shape)],
        out_specs=[head_t(MLA_QK),
                   pl.BlockSpec((1, MLA_HEADS, rb, MLA_QK), lambda b, i: (b, 0, i, 0)),
                   head_t(MLA_DV_AUG)],
        out_shape=[jax.ShapeDtypeStruct((batch, MLA_HEADS, MLA_QK, seq), BF16),
                   jax.ShapeDtypeStruct((batch, MLA_HEADS, seq, MLA_QK), BF16),
                   jax.ShapeDtypeStruct((batch, MLA_HEADS, MLA_DV_AUG, seq), BF16)],
        compiler_params=_params(("parallel", "parallel")),
        name="mla_up_proj",
    )(cq, ckv, kr, cos_t, sin_t, wqt, wk, wvt)


def _flash_kernel(qt_ref, k_ref, vt_ref, o_ref, m_scr, acc_scr, sa_scr, sb_scr, *, tq, tk):
    i = pl.program_id(2)
    ratio = tq // tk
    m_scr[...] = jnp.full(m_scr.shape, -jnp.inf, F32)
    acc_scr[...] = jnp.zeros(acc_scr.shape, F32)

    heads = range(ATTN_HEADS_PER_STEP)

    def scores(j, hd):
        start = pl.multiple_of(j * tk, tk)
        return _dot(k_ref[0, hd, pl.ds(start, tk), :], qt_ref[0, hd])

    def update(s, j, hd, diag):
        start = pl.multiple_of(j * tk, tk)
        vtb = vt_ref[0, hd, :, pl.ds(start, tk)]
        if diag is not None:
            kc = (diag * tk + lax.broadcasted_iota(jnp.int32, (tk, tq), 0)) // CHUNK
            qc = lax.broadcasted_iota(jnp.int32, (tk, tq), 1) // CHUNK
            s = jnp.where(kc <= qc, s, NEG_BIG)
        m_prev = m_scr[hd]
        m_new = jnp.maximum(m_prev, jnp.max(s, axis=0, keepdims=True))
        alpha = jnp.exp2(m_prev - m_new)
        p = jnp.exp2(s - m_new)
        acc_scr[hd] = alpha * acc_scr[hd] + _dot(vtb, p.astype(BF16))
        m_scr[hd] = m_new

    def stage(s_cur, s_nxt, j, diag=None):
        for hd in heads:
            s_nxt[hd] = scores(j + 1, hd)
            update(s_cur[hd], j, hd, diag)

    for hd in heads:
        sa_scr[hd] = scores(0, hd)

    def body(jj, carry):
        stage(sa_scr, sb_scr, 2 * jj)
        stage(sb_scr, sa_scr, 2 * jj + 1)
        return carry

    nfull = i * ratio
    lax.fori_loop(0, i * (ratio // 2), body, 0)
    bufs = (sa_scr, sb_scr)
    for d in range(ratio - 1):
        stage(bufs[d % 2], bufs[(d + 1) % 2], nfull + d, d)
    for hd in heads:
        update(bufs[(ratio - 1) % 2][hd], nfull + ratio - 1, hd, ratio - 1)

    for hd in heads:
        out_t = acc_scr[hd, 0:MLA_DV] / acc_scr[hd, MLA_DV:MLA_DV + 1]
        o_ref[0, :, hd * MLA_DV:(hd + 1) * MLA_DV] = jnp.transpose(out_t).astype(BF16)


def _flash(qt, k, vt):
    batch, heads, seq, _ = k.shape
    tq = min(ATTN_Q_TILE, seq)
    tk = min(ATTN_K_TILE, seq // 2)
    g = ATTN_HEADS_PER_STEP
    assert seq % tq == 0 and tq % (2 * tk) == 0 and tk % CHUNK == 0 and heads % g == 0
    return pl.pallas_call(
        functools.partial(_flash_kernel, tq=tq, tk=tk),
        grid=(batch, heads // g, seq // tq),
        in_specs=[pl.BlockSpec((1, g, MLA_QK, tq), lambda b, h, i: (b, h, 0, i)),
                  pl.BlockSpec((1, g, seq, MLA_QK), lambda b, h, i: (b, h, 0, 0)),
                  pl.BlockSpec((1, g, MLA_DV_AUG, seq), lambda b, h, i: (b, h, 0, 0))],
        out_specs=pl.BlockSpec((1, tq, g * MLA_DV), lambda b, h, i: (b, i, h)),
        out_shape=jax.ShapeDtypeStruct((batch, seq, MLA_WIDTH), BF16),
        scratch_shapes=[pltpu.VMEM((g, 1, tq), F32),
                        pltpu.VMEM((g, MLA_DV_AUG, tq), F32), pltpu.VMEM((g, tk, tq), F32),
                        pltpu.VMEM((g, tk, tq), F32)],
        compiler_params=_params(("parallel", "parallel", "arbitrary")),
        name="mla_flash",
    )(qt, k, vt)


def _cached_attn_kernel(cq_ref, ckvn_ref, krn_ref, ckvp_ref, krp_ref, cos_ref, sin_ref,
                        wqn_ref, wqr_ref, wkt_ref, wv_ref, o_ref, *, tq, past):
    cq = cq_ref[0]
    lane = lax.broadcasted_iota(jnp.int32, (tq, LANES), 1)
    first_half = (lane % 64) < 32
    qscale = MLA_SCALE * LOG2E
    qn = (_dot(cq, wqn_ref[...]) * qscale).astype(BF16)
    qr = (_rope_cols(_dot(cq, wqr_ref[...]), cos_ref[...], sin_ref[...], first_half) * qscale).astype(BF16)
    qlat = jnp.concatenate(
        [_dot(qn[:, hd * MLA_NOPE:(hd + 1) * MLA_NOPE], wkt_ref[hd]) for hd in range(MLA_HEADS)],
        axis=0).astype(BF16)
    qrr = jnp.concatenate([qr[:, hd * MLA_ROPE:(hd + 1) * MLA_ROPE] for hd in range(MLA_HEADS)], axis=0)
    ckv_p = ckvp_ref[0].astype(BF16)
    kr_p = krp_ref[0].astype(BF16)
    ckv_n = ckvn_ref[0].astype(BF16)
    kr_n = krn_ref[0].astype(BF16)
    s_p = _dot_nt(qlat, ckv_p) + _dot_nt(qrr, kr_p)
    s_n = _dot_nt(qlat, ckv_n) + _dot_nt(qrr, kr_n)
    rows = MLA_HEADS * tq
    qpos = past + lax.broadcasted_iota(jnp.int32, (rows, tq), 0) % tq
    kpos = past + lax.broadcasted_iota(jnp.int32, (rows, tq), 1)
    s_n = jnp.where(kpos // CHUNK <= qpos // CHUNK, s_n, NEG_BIG)
    m = jnp.maximum(jnp.max(s_p, axis=1, keepdims=True), jnp.max(s_n, axis=1, keepdims=True))
    p_p = jnp.exp2(s_p - m)
    p_n = jnp.exp2(s_n - m)
    l = jnp.sum(p_p, axis=1, keepdims=True) + jnp.sum(p_n, axis=1, keepdims=True)
    olat = (_dot(p_p.astype(BF16), ckv_p) + _dot(p_n.astype(BF16), ckv_n)) / l
    olat = olat.astype(BF16)
    for hd in range(MLA_HEADS):
        o_ref[0, :, hd * MLA_DV:(hd + 1) * MLA_DV] = _dot(olat[hd * tq:(hd + 1) * tq], wv_ref[hd]).astype(BF16)


def _cached_attn(cq, ckv_new, kr_new, ckv_past, kr_past, cos, sin, wqn, wqr, wkt, wvh):
    batch, tq, _ = cq.shape
    past = ckv_past.shape[1]
    assert past % CHUNK == 0
    one = lambda n, w: pl.BlockSpec((1, n, w), lambda b: (b, 0, 0))
    return pl.pallas_call(
        functools.partial(_cached_attn_kernel, tq=tq, past=past),
        grid=(batch,),
        in_specs=[one(tq, Q_LORA), one(tq, KV_LORA), one(tq, MLA_ROPE), one(past, KV_LORA), one(past, MLA_ROPE),
                  _const_spec(cos.shape), _const_spec(sin.shape), _const_spec(wqn.shape), _const_spec(wqr.shape),
                  _const_spec(wkt.shape), _const_spec(wvh.shape)],
        out_specs=one(tq, MLA_WIDTH),
        out_shape=jax.ShapeDtypeStruct((batch, tq, MLA_WIDTH), BF16),
        compiler_params=_params(("parallel",)),
        name="mla_cached",
    )(cq, ckv_new, kr_new, ckv_past, kr_past, cos, sin, wqn, wqr, wkt, wvh)


def _layernorm(y, g, b):
    mu = jnp.mean(y, axis=-1, keepdims=True)
    d = y - mu
    var = jnp.mean(d * d, axis=-1, keepdims=True)
    return d * lax.rsqrt(var + EPS) * g + b


def _outproj_kernel(oret_ref, omla_ref, x_ref, g1_ref, w_ref, lng_ref, lnb_ref, x1_ref, *, nb, rb):
    for half in range(2):
        if nb > 1:
            bsl, rsl, hb, hr = slice(half * nb // 2, (half + 1) * nb // 2), slice(None), nb // 2, rb
        else:
            bsl, rsl, hb, hr = slice(None), slice(half * rb // 2, (half + 1) * rb // 2), nb, rb // 2
        a = oret_ref[bsl, rsl, :].reshape(hb * hr, RET_WIDTH)
        b = omla_ref[bsl, rsl, :].reshape(hb * hr, MLA_WIDTH)
        mix = _dot(a, w_ref[0:RET_WIDTH, :]) + _dot(b, w_ref[RET_WIDTH:RET_WIDTH + MLA_WIDTH, :])
        y = ALPHA * x_ref[bsl, rsl, :] + g1_ref[bsl] * mix.reshape(hb, hr, D_MODEL)
        x1_ref[bsl, rsl, :] = _layernorm(y, lng_ref[...], lnb_ref[...])


def _outproj(o_ret, o_mla, x, g1, w_out_b, ln_g, ln_b):
    batch, seq, _ = x.shape
    nb, rb = _row_tiling(batch, seq, ROW_TILE)
    row = lambda w: pl.BlockSpec((nb, rb, w), lambda b, i: (b, i, 0))
    mod = pl.BlockSpec((nb, 1, D_MODEL), lambda b, i: (b, 0, 0))
    return pl.pallas_call(
        functools.partial(_outproj_kernel, nb=nb, rb=rb),
        grid=(batch // nb, seq // rb),
        in_specs=[row(RET_WIDTH), row(MLA_WIDTH), row(D_MODEL), mod, _const_spec(w_out_b.shape),
                  _const_spec((1, 1, D_MODEL)), _const_spec((1, 1, D_MODEL))],
        out_specs=row(D_MODEL),
        out_shape=jax.ShapeDtypeStruct((batch, seq, D_MODEL), F32),
        compiler_params=_params(("parallel", "parallel")),
        name="out_proj_ln1",
    )(o_ret, o_mla, x, g1, w_out_b, ln_g, ln_b)


def _ffn_kernel(x1_ref, sc_ref, sh_ref, g2_ref, wg_ref, wu_ref, wd_ref, lng_ref, lnb_ref, o_ref, h_scr,
                *, nb, rb):
    m = nb * rb
    j = pl.program_id(2)

    @pl.when(j == 0)
    def _():
        x1 = x1_ref[...]
        h_scr[...] = (x1 * (1.0 + sc_ref[...]) + sh_ref[...]).astype(BF16).reshape(m, D_MODEL)
        o_ref[...] = ALPHA * x1

    h = h_scr[...]
    act = (_silu(_dot(h, wg_ref[...])) * _dot(h, wu_ref[...])).astype(BF16)
    for n in range(D_MODEL // FFN_COL_TILE):
        cols = slice(n * FFN_COL_TILE, (n + 1) * FFN_COL_TILE)
        part = _dot(act, wd_ref[:, cols]).reshape(nb, rb, FFN_COL_TILE)
        o_ref[:, :, cols] += part * g2_ref[:, :, cols]

    @pl.when(j == pl.num_programs(2) - 1)
    def _():
        o_ref[...] = _layernorm(o_ref[...], lng_ref[...], lnb_ref[...])


def _ffn(x1, sc, sh, g2, wg, wu, wd, ln_g, ln_b):
    batch, seq, _ = x1.shape
    nb, rb = _row_tiling(batch, seq, FFN_ROW_TILE)
    tf = FFN_COL_TILE
    assert D_FF % tf == 0
    row = pl.BlockSpec((nb, rb, D_MODEL), lambda b, i, j: (b, i, 0))
    mod = pl.BlockSpec((nb, 1, D_MODEL), lambda b, i, j: (b, 0, 0))
    return pl.pallas_call(
        functools.partial(_ffn_kernel, nb=nb, rb=rb),
        grid=(batch // nb, seq // rb, D_FF // tf),
        in_specs=[row, mod, mod, mod,
                  pl.BlockSpec((D_MODEL, tf), lambda b, i, j: (0, j)),
                  pl.BlockSpec((D_MODEL, tf), lambda b, i, j: (0, j)),
                  pl.BlockSpec((tf, D_MODEL), lambda b, i, j: (j, 0)),
                  _const_spec((1, 1, D_MODEL)), _const_spec((1, 1, D_MODEL))],
        out_specs=row,
        out_shape=jax.ShapeDtypeStruct((batch, seq, D_MODEL), F32),
        scratch_shapes=[pltpu.VMEM((nb * rb, D_MODEL), BF16)],
        compiler_params=_params(("parallel", "parallel", "arbitrary"), FFN_VMEM_LIMIT),
        name="swiglu_ln2",
    )(x1, sc, sh, g2, wg, wu, wd, ln_g, ln_b)


def _layer(x, mods, pos0, ckv_past, kr_past, ret_s0, wts):
    sh1, sc1, g1, sh2, sc2, g2 = mods
    seq = x.shape[1]
    cos, sin = _rope_tables(pos0, seq)
    rq, rk, rv, rg, cq, ckv, kr = _inproj(x, sc1, sh1, wts["w_in"], wts["g_cq"], wts["g_ckv"], cos, sin)
    o_ret, ret_s1 = _retention(rq, rk, rv, rg, ret_s0, wts["g_ret"], wts["b_ret"])
    if ckv_past is None:
        cos_t, sin_t = _rope_tables_t(pos0, seq)
        qt, k, vt = _upproj(cq, ckv, kr, cos_t, sin_t, wts["wqt"], wts["wk"], wts["wvt"])
        o_mla = _flash(qt, k, vt)
    else:
        o_mla = _cached_attn(cq, ckv, kr, ckv_past, kr_past, cos, sin,
                             wts["wqn"], wts["wqr"], wts["wkt"], wts["wvh"])
    x1 = _outproj(o_ret, o_mla, x, g1, wts["w_out"], wts["ln1_g"], wts["ln1_b"])
    y = _ffn(x1, sc2, sh2, g2, wts["w_gate"], wts["w_up"], wts["w_down"], wts["ln2_g"], wts["ln2_b"])
    return y, ckv, kr, ret_s1


def _prep_weights(l, w_in, g_cq, g_ckv, w_uq, w_uk, w_uv, g_ret, b_ret, w_out, ln1_g, ln1_b,
                  w_gate, w_up, w_down, ln2_g, ln2_b):
    kr_lo = 2 * RET_QK + 2 * RET_WIDTH + Q_LORA + KV_LORA
    w_in_l = w_in[l]
    return {
        "w_in": jnp.concatenate([w_in_l, w_in_l[:, kr_lo:kr_lo + MLA_ROPE]], axis=1).astype(BF16),
        "g_cq": g_cq[l].reshape(1, Q_LORA),
        "g_ckv": g_ckv[l].reshape(1, KV_LORA),
        "wqn": w_uq[l][:, :, :MLA_NOPE].reshape(Q_LORA, MLA_HEADS * MLA_NOPE).astype(BF16),
        "wqr": w_uq[l][:, :, MLA_NOPE:].reshape(Q_LORA, MLA_HEADS * MLA_ROPE).astype(BF16),
        "wk": w_uk[l].reshape(KV_LORA, MLA_HEADS * MLA_NOPE).astype(BF16),
        "wqt": jnp.transpose(w_uq[l].reshape(Q_LORA, MLA_HEADS * MLA_QK)).astype(BF16),
        "wvt": jnp.transpose(w_uv[l].reshape(KV_LORA, MLA_HEADS * MLA_DV)).astype(BF16),
        "wkt": jnp.transpose(w_uk[l], (1, 2, 0)).astype(BF16),
        "wvh": jnp.transpose(w_uv[l], (1, 0, 2)).astype(BF16),
        "g_ret": g_ret[l].reshape(1, RET_WIDTH),
        "b_ret": b_ret[l].reshape(1, RET_WIDTH),
        "w_out": w_out[l].astype(BF16),
        "ln1_g": ln1_g[l].reshape(1, 1, D_MODEL),
        "ln1_b": ln1_b[l].reshape(1, 1, D_MODEL),
        "w_gate": w_gate[l].astype(BF16),
        "w_up": w_up[l].astype(BF16),
        "w_down": w_down[l].astype(BF16),
        "ln2_g": ln2_g[l].reshape(1, 1, D_MODEL),
        "ln2_b": ln2_b[l].reshape(1, 1, D_MODEL),
    }


def kernel(x_prompt, x_sample, c_prompt, c_sample, cache_mla_ckv, cache_mla_krope, state_ret, w_ada, b_ada, w_in, g_cq, g_ckv, w_uq, w_uk, w_uv, g_ret, b_ret, w_out, ln1_g, ln1_b, w_gate, w_up, w_down, ln2_g, ln2_b):
    bp, tp, _ = x_prompt.shape
    bs, ts, _ = x_sample.shape
    past = cache_mla_ckv.shape[2]
    c_all = jnp.concatenate([c_prompt, c_sample], axis=0)
    yp, ys = x_prompt, x_sample
    outs = [[] for _ in range(6)]
    for l in range(w_ada.shape[0]):
        wts = _prep_weights(l, w_in, g_cq, g_ckv, w_uq, w_uk, w_uv, g_ret, b_ret, w_out, ln1_g, ln1_b,
                            w_gate, w_up, w_down, ln2_g, ln2_b)
        mod = _ada(c_all, w_ada[l], b_ada[l]).reshape(bp + bs, 6, 1, D_MODEL)
        mods_p = tuple(mod[:bp, i] for i in range(6))
        mods_s = tuple(mod[bp:, i] for i in range(6))
        s0 = jnp.zeros((bp, RET_HEADS, RET_DK, RET_DV), F32)
        yp, ckv_p, kr_p, rs_p = _layer(yp, mods_p, 0, None, None, s0, wts)
        ys, ckv_s, kr_s, rs_s = _layer(ys, mods_s, past, cache_mla_ckv[l], cache_mla_krope[l],
                                       state_ret[l], wts)
        for lst, val in zip(outs, (ckv_p, kr_p, rs_p, ckv_s, kr_s, rs_s)):
            lst.append(val)
    stacked = [v[0][None] if len(v) == 1 else jnp.stack(v, axis=0) for v in outs]
    return (yp, ys, *stacked)
```

```python
import functools
import math

import jax
import jax.numpy as jnp
import numpy as np
from jax import lax
from jax.experimental import pallas as pl
from jax.experimental.pallas import tpu as pltpu

F32 = jnp.float32
BF16 = jnp.bfloat16

D_MODEL = 2048
CHUNK = 64
RET_HEADS = 8
RET_DK = 64
RET_DV = 128
RET_QK = RET_HEADS * RET_DK
RET_WIDTH = RET_HEADS * RET_DV
MLA_HEADS = 8
MLA_NOPE = 128
MLA_ROPE = 64
MLA_DV = 128
MLA_DV_AUG = MLA_DV + 16
MLA_QK = MLA_NOPE + MLA_ROPE
MLA_WIDTH = MLA_HEADS * MLA_DV
Q_LORA = 512
KV_LORA = 512
MLA_SCALE = MLA_QK ** -0.5
D_FF = 5632
DEPTH = 1
ALPHA = (2 * DEPTH) ** 0.25
ROPE_BASE = 10000.0
EPS = 1e-5
LOG2E = math.log2(math.e)
NEG_BIG = -1e30

LANES = 128
ROW_TILE = 512
FFN_ROW_TILE = 1024
FFN_COL_TILE = 512
RET_CHUNK = 256
ATTN_Q_TILE = 1024
ATTN_K_TILE = 512
ATTN_HEADS_PER_STEP = 2
ADA_COL_TILE = 1024
VMEM_LIMIT = 56 * 1024 * 1024
FFN_VMEM_LIMIT = 60 * 1024 * 1024


def _params(sem, vmem=VMEM_LIMIT):
    return pltpu.CompilerParams(dimension_semantics=sem, vmem_limit_bytes=vmem)


def _const_spec(shape):
    zeros = (0,) * len(shape)
    return pl.BlockSpec(shape, lambda *_: zeros, pipeline_mode=pl.Buffered(1))


def _dot(a, b):
    return jnp.dot(a, b, preferred_element_type=F32)


def _dot_nt(a, b):
    return lax.dot_general(a, b, (((1,), (1,)), ((), ())), preferred_element_type=F32)


def _dot_tn(a, b):
    return lax.dot_general(a, b, (((0,), (0,)), ((), ())), preferred_element_type=F32)


def _silu(x):
    return x * jax.nn.sigmoid(x)


def _row_tiling(batch, seq, rows):
    if seq >= rows:
        assert seq % rows == 0
        return 1, rows
    nb = min(batch, rows // seq)
    assert batch % nb == 0
    return nb, seq


def _rope_cols(y, cos, sin, first_half):
    outs = []
    for g in range(y.shape[1] // LANES):
        yg = y[:, g * LANES:(g + 1) * LANES]
        rot = jnp.where(first_half, pltpu.roll(yg, LANES - 32, 1), pltpu.roll(yg, 32, 1))
        outs.append(yg * cos + rot * sin)
    return outs[0] if len(outs) == 1 else jnp.concatenate(outs, axis=1)


def _rope_angles(pos0, n):
    half = 32
    inv = ROPE_BASE ** (-np.arange(half, dtype=np.float64) / half)
    pos = (pos0 + np.arange(n)).astype(np.float64)
    return pos[:, None] * inv[None, :]


def _rope_tables(pos0, n):
    ang = _rope_angles(pos0, n)
    cos = np.cos(ang).astype(np.float32)
    sin = np.sin(ang).astype(np.float32)
    return np.concatenate([cos, cos, cos, cos], axis=1), np.concatenate([-sin, sin, -sin, sin], axis=1)


def _rope_tables_t(pos0, n):
    ang = _rope_angles(pos0, n)
    return (np.ascontiguousarray(np.cos(ang).T.astype(np.float32)),
            np.ascontiguousarray(np.sin(ang).T.astype(np.float32)))


def _tile_rows(t, nb):
    if nb == 1:
        return t
    return jnp.broadcast_to(t[None], (nb,) + t.shape).reshape(nb * t.shape[0], t.shape[1])


def _ada_kernel(c_ref, w_ref, b_ref, o_ref):
    c = c_ref[...]
    a = _silu(c).astype(BF16)
    o_ref[...] = _dot(a, w_ref[...].astype(BF16)) + b_ref[...]


def _ada(c, w_ada, b_ada):
    nrow = c.shape[0]
    ncol = w_ada.shape[1]
    tn = ADA_COL_TILE
    return pl.pallas_call(
        _ada_kernel,
        grid=(ncol // tn,),
        in_specs=[_const_spec((nrow, D_MODEL)),
                  pl.BlockSpec((D_MODEL, tn), lambda j: (0, j)),
                  pl.BlockSpec((1, tn), lambda j: (0, j))],
        out_specs=pl.BlockSpec((nrow, tn), lambda j: (0, j)),
        out_shape=jax.ShapeDtypeStruct((nrow, ncol), F32),
        compiler_params=_params(("arbitrary",)),
        name="ada_mod",
    )(c, w_ada, b_ada.reshape(1, ncol))


IN_W = 2 * RET_QK + 2 * RET_WIDTH + Q_LORA + KV_LORA + MLA_ROPE


def _inproj_kernel(x_ref, sc_ref, sh_ref, w_ref, gcq_ref, gckv_ref, cos_ref, sin_ref,
                   rq_ref, rk_ref, rv_ref, rg_ref, cq_ref, ckv_ref, kr_ref, *, nb, rb):
    m = nb * rb
    h = x_ref[...] * (1.0 + sc_ref[...]) + sh_ref[...]
    hb = h.astype(BF16).reshape(m, D_MODEL)
    cos = _tile_rows(cos_ref[...], nb)
    sin = _tile_rows(sin_ref[...], nb)
    lane = lax.broadcasted_iota(jnp.int32, (m, LANES), 1)
    first_half = (lane % 64) < 32

    def proj(lo, hi):
        return _dot(hb, w_ref[:, lo:hi])

    def rmsnorm(y, g):
        ms = jnp.mean(y * y, axis=-1, keepdims=True)
        return y * lax.rsqrt(ms + EPS) * g

    o = 0
    rq = _rope_cols(proj(o, o + RET_QK), cos, sin, first_half) * (RET_DK ** -0.5)
    rq_ref[...] = rq.astype(BF16).reshape(nb, rb, RET_QK)
    o += RET_QK
    rk = _rope_cols(proj(o, o + RET_QK), cos, sin, first_half)
    rk_ref[...] = rk.astype(BF16).reshape(nb, rb, RET_QK)
    o += RET_QK
    rv_ref[...] = proj(o, o + RET_WIDTH).astype(BF16).reshape(nb, rb, RET_WIDTH)
    o += RET_WIDTH
    rg_ref[...] = proj(o, o + RET_WIDTH).astype(BF16).reshape(nb, rb, RET_WIDTH)
    o += RET_WIDTH
    cq_ref[...] = rmsnorm(proj(o, o + Q_LORA), gcq_ref[...]).astype(BF16).reshape(nb, rb, Q_LORA)
    o += Q_LORA
    ckv_ref[...] = rmsnorm(proj(o, o + KV_LORA), gckv_ref[...]).reshape(nb, rb, KV_LORA)
    o += KV_LORA
    kr = proj(o, o + MLA_ROPE)
    kr = _rope_cols(jnp.concatenate([kr, kr], axis=1), cos, sin, first_half)
    kr_ref[...] = kr[:, :MLA_ROPE].reshape(nb, rb, MLA_ROPE)


def _inproj(x, sc, sh, w_in_b, g_cq, g_ckv, cos, sin):
    batch, seq, _ = x.shape
    nb, rb = _row_tiling(batch, seq, ROW_TILE)
    grid = (batch // nb, seq // rb)
    row = lambda w: pl.BlockSpec((nb, rb, w), lambda b, i: (b, i, 0))
    mod = pl.BlockSpec((nb, 1, D_MODEL), lambda b, i: (b, 0, 0))
    tab = pl.BlockSpec((rb, LANES), lambda b, i: (i, 0))
    sds = lambda w, dt: jax.ShapeDtypeStruct((batch, seq, w), dt)
    return pl.pallas_call(
        functools.partial(_inproj_kernel, nb=nb, rb=rb),
        grid=grid,
        in_specs=[row(D_MODEL), mod, mod, _const_spec((D_MODEL, IN_W)),
                  _const_spec((1, Q_LORA)), _const_spec((1, KV_LORA)), tab, tab],
        out_specs=[row(RET_QK), row(RET_QK), row(RET_WIDTH), row(RET_WIDTH),
                   row(Q_LORA), row(KV_LORA), row(MLA_ROPE)],
        out_shape=[sds(RET_QK, BF16), sds(RET_QK, BF16), sds(RET_WIDTH, BF16), sds(RET_WIDTH, BF16),
                   sds(Q_LORA, BF16), sds(KV_LORA, F32), sds(MLA_ROPE, F32)],
        compiler_params=_params(("parallel", "parallel")),
        name="in_proj",
    )(x, sc, sh, w_in_b, g_cq, g_ckv, cos, sin)


def _ret_tables(c):
    hh = np.arange(RET_HEADS, dtype=np.float64)
    logg = np.log1p(-np.exp2(-5.0 - hh))
    i = np.arange(c, dtype=np.float64)
    diff = i[:, None] - i[None, :]
    dmask = np.where(diff[None] >= 0.0, np.exp(np.maximum(diff, 0.0)[None] * logg[:, None, None]), 0.0)
    qdec = np.exp((i + 1.0)[:, None] * logg[None, :])
    kdec = np.exp((c - 1.0 - i)[:, None] * logg[None, :])
    cdec = np.exp(c * logg)
    qdec = np.repeat(qdec, RET_DK, axis=1)
    kdec = np.repeat(kdec, RET_DK, axis=1)
    r = np.arange(2 * RET_DK)[:, None] // RET_DK
    col = np.arange(2 * RET_DV)[None, :] // RET_DV
    pair = np.arange(RET_HEADS // 2)[:, None, None]
    sdec = np.where(r[None] == col[None], cdec[2 * pair + r[None]], 0.0)
    return tuple(a.astype(np.float32) for a in (dmask, qdec, kdec, sdec))


def _ret_kernel(rq_ref, rk_ref, rv_ref, rg_ref, s0_ref, dmask_ref, qdec_ref, kdec_ref, sdec_ref,
                gret_ref, bret_ref, o_ref, so_ref, s_scr, *, c):
    n = pl.program_id(1)
    npair = RET_HEADS // 2

    @pl.when(n == 0)
    def _():
        s_scr[...] = jnp.zeros(s_scr.shape, F32)
        for p in range(npair):
            s_scr[p, 0:RET_DK, 0:RET_DV] = s0_ref[0, 2 * p]
            s_scr[p, RET_DK:2 * RET_DK, RET_DV:2 * RET_DV] = s0_ref[0, 2 * p + 1]

    q = rq_ref[0]
    k = rk_ref[0]
    qd = (q.astype(F32) * qdec_ref[...]).astype(BF16)
    kd = (k.astype(F32) * kdec_ref[...]).astype(BF16)
    lane = lax.broadcasted_iota(jnp.int32, (c, LANES), 1)
    lo = lane < RET_DK
    zero = jnp.zeros((c, LANES), BF16)
    for p in range(npair):
        qp = q[:, p * LANES:(p + 1) * LANES]
        kp = k[:, p * LANES:(p + 1) * LANES]
        vp = rv_ref[0, :, p * 2 * RET_DV:(p + 1) * 2 * RET_DV]
        sc0 = _dot_nt(jnp.where(lo, qp, zero), kp) * dmask_ref[2 * p]
        sc1 = _dot_nt(jnp.where(lo, zero, qp), kp) * dmask_ref[2 * p + 1]
        intra0 = _dot(sc0.astype(BF16), vp[:, :RET_DV])
        intra1 = _dot(sc1.astype(BF16), vp[:, RET_DV:])
        s = s_scr[p]
        inter = _dot(qd[:, p * LANES:(p + 1) * LANES], s.astype(BF16))
        kv = _dot_tn(kd[:, p * LANES:(p + 1) * LANES], vp)
        sdec = sdec_ref[p]
        s_scr[p] = s * sdec + jnp.where(sdec > 0.0, kv, 0.0)
        for hh in range(2):
            hd = 2 * p + hh
            cols = slice(hd * RET_DV, (hd + 1) * RET_DV)
            oh = (intra0 if hh == 0 else intra1) + inter[:, hh * RET_DV:(hh + 1) * RET_DV]
            mu = jnp.mean(oh, axis=-1, keepdims=True)
            d = oh - mu
            var = jnp.mean(d * d, axis=-1, keepdims=True)
            y = d * lax.rsqrt(var + EPS) * gret_ref[:, cols] + bret_ref[:, cols]
            gate = rg_ref[0, :, cols].astype(F32)
            o_ref[0, :, cols] = (_silu(gate) * y).astype(BF16)

    @pl.when(n == pl.num_programs(1) - 1)
    def _():
        for p in range(npair):
            so_ref[0, 2 * p] = s_scr[p, 0:RET_DK, 0:RET_DV]
            so_ref[0, 2 * p + 1] = s_scr[p, RET_DK:2 * RET_DK, RET_DV:2 * RET_DV]


def _retention(rq, rk, rv, rg, s0, g_ret, b_ret):
    batch, seq, _ = rq.shape
    c = min(RET_CHUNK, seq)
    assert seq % c == 0
    dmask, qdec, kdec, sdec = _ret_tables(c)
    row = lambda w: pl.BlockSpec((1, c, w), lambda b, n: (b, n, 0))
    st = pl.BlockSpec((1, RET_HEADS, RET_DK, RET_DV), lambda b, n: (b, 0, 0, 0))
    return pl.pallas_call(
        functools.partial(_ret_kernel, c=c),
        grid=(batch, seq // c),
        in_specs=[row(RET_QK), row(RET_QK), row(RET_WIDTH), row(RET_WIDTH), st,
                  _const_spec(dmask.shape), _const_spec(qdec.shape), _const_spec(kdec.shape),
                  _const_spec(sdec.shape), _const_spec((1, RET_WIDTH)), _const_spec((1, RET_WIDTH))],
        out_specs=[row(RET_WIDTH), st],
        out_shape=[jax.ShapeDtypeStruct((batch, seq, RET_WIDTH), BF16),
                   jax.ShapeDtypeStruct((batch, RET_HEADS, RET_DK, RET_DV), F32)],
        scratch_shapes=[pltpu.VMEM((RET_HEADS // 2, 2 * RET_DK, 2 * RET_DV), F32)],
        compiler_params=_params(("parallel", "arbitrary")),
        name="retention",
    )(rq, rk, rv, rg, s0, dmask, qdec, kdec, sdec, g_ret, b_ret)


def _upproj_kernel(cq_ref, ckv_ref, kr_ref, cos_ref, sin_ref, wqt_ref, wk_ref, wvt_ref,
                   qt_ref, k_ref, vt_ref):
    cq = cq_ref[0]
    ckv = ckv_ref[0].astype(BF16)
    krb = kr_ref[0].astype(BF16)
    cos = cos_ref[...]
    sin = sin_ref[...]
    half = MLA_ROPE // 2
    qt = _dot_nt(wqt_ref[...], cq) * (MLA_SCALE * LOG2E)
    kn = _dot(ckv, wk_ref[...]).astype(BF16)
    vt = _dot_nt(wvt_ref[...], ckv).astype(BF16)
    for hd in range(MLA_HEADS):
        r0 = hd * MLA_QK
        x1 = qt[r0 + MLA_NOPE:r0 + MLA_NOPE + half]
        x2 = qt[r0 + MLA_NOPE + half:r0 + MLA_QK]
        qt_ref[0, hd, 0:MLA_NOPE] = qt[r0:r0 + MLA_NOPE].astype(BF16)
        qt_ref[0, hd, MLA_NOPE:MLA_NOPE + half] = (x1 * cos - x2 * sin).astype(BF16)
        qt_ref[0, hd, MLA_NOPE + half:MLA_QK] = (x2 * cos + x1 * sin).astype(BF16)
        k_ref[0, hd, :, 0:MLA_NOPE] = kn[:, hd * MLA_NOPE:(hd + 1) * MLA_NOPE]
        k_ref[0, hd, :, MLA_NOPE:MLA_QK] = krb
        vt_ref[0, hd, 0:MLA_DV] = vt[hd * MLA_DV:(hd + 1) * MLA_DV]
        vt_ref[0, hd, MLA_DV:MLA_DV_AUG] = jnp.ones((MLA_DV_AUG - MLA_DV, vt.shape[1]), BF16)


def _upproj(cq, ckv, kr, cos_t, sin_t, wqt, wk, wvt):
    batch, seq, _ = cq.shape
    rb = min(ROW_TILE, seq)
    assert seq % rb == 0
    row = lambda w: pl.BlockSpec((1, rb, w), lambda b, i: (b, i, 0))
    tab = pl.BlockSpec((MLA_ROPE // 2, rb), lambda b, i: (0, i))
    head_t = lambda w: pl.BlockSpec((1, MLA_HEADS, w, rb), lambda b, i: (b, 0, 0, i))
    return pl.pallas_call(
        _upproj_kernel,
        grid=(batch, seq // rb),
        in_specs=[row(Q_LORA), row(KV_LORA), row(MLA_ROPE), tab, tab,
                  _const_spec(wqt.shape), _const_spec(wk.shape), _const_spec(wvt.shape)],
        out_specs=[head_t(MLA_QK),
                   pl.BlockSpec((1, MLA_HEADS, rb, MLA_QK), lambda b, i: (b, 0, i, 0)),
                   head_t(MLA_DV_AUG)],
        out_shape=[jax.ShapeDtypeStruct((batch, MLA_HEADS, MLA_QK, seq), BF16),
                   jax.ShapeDtypeStruct((batch, MLA_HEADS, seq, MLA_QK), BF16),
                   jax.ShapeDtypeStruct((batch, MLA_HEADS, MLA_DV_AUG, seq), BF16)],
        compiler_params=_params(("parallel", "parallel")),
        name="mla_up_proj",
    )(cq, ckv, kr, cos_t, sin_t, wqt, wk, wvt)


def _flash_kernel(qt_ref, k_ref, vt_ref, o_ref, m_scr, acc_scr, sa_scr, sb_scr, ma_scr, mb_scr, *, tq, tk):
    i = pl.program_id(2)
    ratio = tq // tk
    m_scr[...] = jnp.full(m_scr.shape, -jnp.inf, F32)
    acc_scr[...] = jnp.zeros(acc_scr.shape, F32)

    heads = range(ATTN_HEADS_PER_STEP)

    def scores(j, hd, c0=0):
        start = pl.multiple_of(j * tk, tk)
        return _dot(k_ref[0, hd, pl.ds(start, tk), :], qt_ref[0, hd, :, c0:])

    def update(buf, j, hd, diag=None):
        c0 = 0 if diag is None else diag * tk
        start = pl.multiple_of(j * tk, tk)
        vtb = vt_ref[0, hd, :, pl.ds(start, tk)]
        s = buf[0][hd, :, c0:]
        if diag is None:
            smax = buf[1][hd]
        else:
            kc = (c0 + lax.broadcasted_iota(jnp.int32, (tk, tq - c0), 0)) // CHUNK
            qc = (c0 + lax.broadcasted_iota(jnp.int32, (tk, tq - c0), 1)) // CHUNK
            s = jnp.where(kc <= qc, s, NEG_BIG)
            smax = jnp.max(s, axis=0, keepdims=True)
        m_prev = m_scr[hd, :, c0:]
        m_new = jnp.maximum(m_prev, smax)
        alpha = jnp.exp2(m_prev - m_new)
        p = jnp.exp2(s - m_new)
        acc_scr[hd, :, c0:] = alpha * acc_scr[hd, :, c0:] + _dot(vtb, p.astype(BF16))
        m_scr[hd, :, c0:] = m_new

    def produce(buf, j, hd, diag=None):
        c0 = 0 if diag is None else diag * tk
        s = scores(j, hd, c0)
        buf[0][hd, :, c0:] = s
        if diag is None:
            buf[1][hd] = jnp.max(s, axis=0, keepdims=True)

    def stage(cur, nxt, j, diag=None, diag_next=None):
        for hd in heads:
            produce(nxt, j + 1, hd, diag_next)
            update(cur, j, hd, diag)

    buf_a = (sa_scr, ma_scr)
    buf_b = (sb_scr, mb_scr)
    for hd in heads:
        produce(buf_a, 0, hd)

    def body(jj, carry):
        stage(buf_a, buf_b, 2 * jj)
        stage(buf_b, buf_a, 2 * jj + 1)
        return carry

    nfull = i * ratio
    lax.fori_loop(0, i * (ratio // 2), body, 0)
    bufs = (buf_a, buf_b)
    for d in range(ratio - 1):
        stage(bufs[d % 2], bufs[(d + 1) % 2], nfull + d, d, d + 1)
    for hd in heads:
        update(bufs[(ratio - 1) % 2], nfull + ratio - 1, hd, ratio - 1)

    for hd in heads:
        out_t = acc_scr[hd, 0:MLA_DV] / acc_scr[hd, MLA_DV:MLA_DV + 1]
        o_ref[0, :, hd * MLA_DV:(hd + 1) * MLA_DV] = jnp.transpose(out_t).astype(BF16)


def _flash(qt, k, vt):
    batch, heads, seq, _ = k.shape
    tq = min(ATTN_Q_TILE, seq)
    tk = min(ATTN_K_TILE, seq // 2)
    g = ATTN_HEADS_PER_STEP
    assert seq % tq == 0 and tq % (2 * tk) == 0 and tk % CHUNK == 0 and heads % g == 0
    return pl.pallas_call(
        functools.partial(_flash_kernel, tq=tq, tk=tk),
        grid=(batch, heads // g, seq // tq),
        in_specs=[pl.BlockSpec((1, g, MLA_QK, tq), lambda b, h, i: (b, h, 0, i)),
                  pl.BlockSpec((1, g, seq, MLA_QK), lambda b, h, i: (b, h, 0, 0)),
                  pl.BlockSpec((1, g, MLA_DV_AUG, seq), lambda b, h, i: (b, h, 0, 0))],
        out_specs=pl.BlockSpec((1, tq, g * MLA_DV), lambda b, h, i: (b, i, h)),
        out_shape=jax.ShapeDtypeStruct((batch, seq, MLA_WIDTH), BF16),
        scratch_shapes=[pltpu.VMEM((g, 1, tq), F32),
                        pltpu.VMEM((g, MLA_DV_AUG, tq), F32), pltpu.VMEM((g, tk, tq), F32),
                        pltpu.VMEM((g, tk, tq), F32), pltpu.VMEM((g, 1, tq), F32), pltpu.VMEM((g, 1, tq), F32)],
        compiler_params=_params(("parallel", "parallel", "arbitrary")),
        name="mla_flash",
    )(qt, k, vt)


def _cached_attn_kernel(cq_ref, ckvn_ref, krn_ref, ckvp_ref, krp_ref, cos_ref, sin_ref,
                        wqn_ref, wqr_ref, wkt_ref, wv_ref, o_ref, *, tq, past):
    cq = cq_ref[0]
    lane = lax.broadcasted_iota(jnp.int32, (tq, LANES), 1)
    first_half = (lane % 64) < 32
    qscale = MLA_SCALE * LOG2E
    qn = (_dot(cq, wqn_ref[...]) * qscale).astype(BF16)
    qr = (_rope_cols(_dot(cq, wqr_ref[...]), cos_ref[...], sin_ref[...], first_half) * qscale).astype(BF16)
    qlat = jnp.concatenate(
        [_dot(qn[:, hd * MLA_NOPE:(hd + 1) * MLA_NOPE], wkt_ref[hd]) for hd in range(MLA_HEADS)],
        axis=0).astype(BF16)
    qrr = jnp.concatenate([qr[:, hd * MLA_ROPE:(hd + 1) * MLA_ROPE] for hd in range(MLA_HEADS)], axis=0)
    ckv_p = ckvp_ref[0].astype(BF16)
    kr_p = krp_ref[0].astype(BF16)
    ckv_n = ckvn_ref[0].astype(BF16)
    kr_n = krn_ref[0].astype(BF16)
    s_p = _dot_nt(qlat, ckv_p) + _dot_nt(qrr, kr_p)
    s_n = _dot_nt(qlat, ckv_n) + _dot_nt(qrr, kr_n)
    rows = MLA_HEADS * tq
    qpos = past + lax.broadcasted_iota(jnp.int32, (rows, tq), 0) % tq
    kpos = past + lax.broadcasted_iota(jnp.int32, (rows, tq), 1)
    s_n = jnp.where(kpos // CHUNK <= qpos // CHUNK, s_n, NEG_BIG)
    m = jnp.maximum(jnp.max(s_p, axis=1, keepdims=True), jnp.max(s_n, axis=1, keepdims=True))
    p_p = jnp.exp2(s_p - m)
    p_n = jnp.exp2(s_n - m)
    l = jnp.sum(p_p, axis=1, keepdims=True) + jnp.sum(p_n, axis=1, keepdims=True)
    olat = (_dot(p_p.astype(BF16), ckv_p) + _dot(p_n.astype(BF16), ckv_n)) / l
    olat = olat.astype(BF16)
    for hd in range(MLA_HEADS):
        o_ref[0, :, hd * MLA_DV:(hd + 1) * MLA_DV] = _dot(olat[hd * tq:(hd + 1) * tq], wv_ref[hd]).astype(BF16)


def _cached_attn(cq, ckv_new, kr_new, ckv_past, kr_past, cos, sin, wqn, wqr, wkt, wvh):
    batch, tq, _ = cq.shape
    past = ckv_past.shape[1]
    assert past % CHUNK == 0
    one = lambda n, w: pl.BlockSpec((1, n, w), lambda b: (b, 0, 0))
    return pl.pallas_call(
        functools.partial(_cached_attn_kernel, tq=tq, past=past),
        grid=(batch,),
        in_specs=[one(tq, Q_LORA), one(tq, KV_LORA), one(tq, MLA_ROPE), one(past, KV_LORA), one(past, MLA_ROPE),
                  _const_spec(cos.shape), _const_spec(sin.shape), _const_spec(wqn.shape), _const_spec(wqr.shape),
                  _const_spec(wkt.shape), _const_spec(wvh.shape)],
        out_specs=one(tq, MLA_WIDTH),
        out_shape=jax.ShapeDtypeStruct((batch, tq, MLA_WIDTH), BF16),
        compiler_params=_params(("parallel",)),
        name="mla_cached",
    )(cq, ckv_new, kr_new, ckv_past, kr_past, cos, sin, wqn, wqr, wkt, wvh)


def _layernorm(y, g, b):
    mu = jnp.mean(y, axis=-1, keepdims=True)
    d = y - mu
    var = jnp.mean(d * d, axis=-1, keepdims=True)
    return d * lax.rsqrt(var + EPS) * g + b


def _outproj_kernel(oret_ref, omla_ref, x_ref, g1_ref, w_ref, lng_ref, lnb_ref, x1_ref, *, nb, rb):
    for half in range(2):
        if nb > 1:
            bsl, rsl, hb, hr = slice(half * nb // 2, (half + 1) * nb // 2), slice(None), nb // 2, rb
        else:
            bsl, rsl, hb, hr = slice(None), slice(half * rb // 2, (half + 1) * rb // 2), nb, rb // 2
        a = oret_ref[bsl, rsl, :].reshape(hb * hr, RET_WIDTH)
        b = omla_ref[bsl, rsl, :].reshape(hb * hr, MLA_WIDTH)
        mix = _dot(a, w_ref[0:RET_WIDTH, :]) + _dot(b, w_ref[RET_WIDTH:RET_WIDTH + MLA_WIDTH, :])
        y = ALPHA * x_ref[bsl, rsl, :] + g1_ref[bsl] * mix.reshape(hb, hr, D_MODEL)
        x1_ref[bsl, rsl, :] = _layernorm(y, lng_ref[...], lnb_ref[...])


def _outproj(o_ret, o_mla, x, g1, w_out_b, ln_g, ln_b):
    batch, seq, _ = x.shape
    nb, rb = _row_tiling(batch, seq, ROW_TILE)
    row = lambda w: pl.BlockSpec((nb, rb, w), lambda b, i: (b, i, 0))
    mod = pl.BlockSpec((nb, 1, D_MODEL), lambda b, i: (b, 0, 0))
    return pl.pallas_call(
        functools.partial(_outproj_kernel, nb=nb, rb=rb),
        grid=(batch // nb, seq // rb),
        in_specs=[row(RET_WIDTH), row(MLA_WIDTH), row(D_MODEL), mod, _const_spec(w_out_b.shape),
                  _const_spec((1, 1, D_MODEL)), _const_spec((1, 1, D_MODEL))],
        out_specs=row(D_MODEL),
        out_shape=jax.ShapeDtypeStruct((batch, seq, D_MODEL), F32),
        compiler_params=_params(("parallel", "parallel")),
        name="out_proj_ln1",
    )(o_ret, o_mla, x, g1, w_out_b, ln_g, ln_b)


def _ffn_kernel(x1_ref, sc_ref, sh_ref, g2_ref, wg_ref, wu_ref, wd_ref, lng_ref, lnb_ref, o_ref, h_scr,
                *, nb, rb):
    m = nb * rb
    j = pl.program_id(2)

    @pl.when(j == 0)
    def _():
        x1 = x1_ref[...]
        h_scr[...] = (x1 * (1.0 + sc_ref[...]) + sh_ref[...]).astype(BF16).reshape(m, D_MODEL)
        o_ref[...] = ALPHA * x1

    h = h_scr[...]
    act = (_silu(_dot(h, wg_ref[...])) * _dot(h, wu_ref[...])).astype(BF16)
    for n in range(D_MODEL // FFN_COL_TILE):
        cols = slice(n * FFN_COL_TILE, (n + 1) * FFN_COL_TILE)
        part = _dot(act, wd_ref[:, cols]).reshape(nb, rb, FFN_COL_TILE)
        o_ref[:, :, cols] += part * g2_ref[:, :, cols]

    @pl.when(j == pl.num_programs(2) - 1)
    def _():
        o_ref[...] = _layernorm(o_ref[...], lng_ref[...], lnb_ref[...])


def _ffn(x1, sc, sh, g2, wg, wu, wd, ln_g, ln_b):
    batch, seq, _ = x1.shape
    nb, rb = _row_tiling(batch, seq, FFN_ROW_TILE)
    tf = FFN_COL_TILE
    assert D_FF % tf == 0
    row = pl.BlockSpec((nb, rb, D_MODEL), lambda b, i, j: (b, i, 0))
    mod = pl.BlockSpec((nb, 1, D_MODEL), lambda b, i, j: (b, 0, 0))
    return pl.pallas_call(
        functools.partial(_ffn_kernel, nb=nb, rb=rb),
        grid=(batch // nb, seq // rb, D_FF // tf),
        in_specs=[row, mod, mod, mod,
                  pl.BlockSpec((D_MODEL, tf), lambda b, i, j: (0, j)),
                  pl.BlockSpec((D_MODEL, tf), lambda b, i, j: (0, j)),
                  pl.BlockSpec((tf, D_MODEL), lambda b, i, j: (j, 0)),
                  _const_spec((1, 1, D_MODEL)), _const_spec((1, 1, D_MODEL))],
        out_specs=row,
        out_shape=jax.ShapeDtypeStruct((batch, seq, D_MODEL), F32),
        scratch_shapes=[pltpu.VMEM((nb * rb, D_MODEL), BF16)],
        compiler_params=_params(("parallel", "parallel", "arbitrary"), FFN_VMEM_LIMIT),
        name="swiglu_ln2",
    )(x1, sc, sh, g2, wg, wu, wd, ln_g, ln_b)


def _layer(x, mods, pos0, ckv_past, kr_past, ret_s0, wts):
    sh1, sc1, g1, sh2, sc2, g2 = mods
    seq = x.shape[1]
    cos, sin = _rope_tables(pos0, seq)
    rq, rk, rv, rg, cq, ckv, kr = _inproj(x, sc1, sh1, wts["w_in"], wts["g_cq"], wts["g_ckv"], cos, sin)
    o_ret, ret_s1 = _retention(rq, rk, rv, rg, ret_s0, wts["g_ret"], wts["b_ret"])
    if ckv_past is None:
        cos_t, sin_t = _rope_tables_t(pos0, seq)
        qt, k, vt = _upproj(cq, ckv, kr, cos_t, sin_t, wts["wqt"], wts["wk"], wts["wvt"])
        o_mla = _flash(qt, k, vt)
    else:
        o_mla = _cached_attn(cq, ckv, kr, ckv_past, kr_past, cos, sin,
                             wts["wqn"], wts["wqr"], wts["wkt"], wts["wvh"])
    x1 = _outproj(o_ret, o_mla, x, g1, wts["w_out"], wts["ln1_g"], wts["ln1_b"])
    y = _ffn(x1, sc2, sh2, g2, wts["w_gate"], wts["w_up"], wts["w_down"], wts["ln2_g"], wts["ln2_b"])
    return y, ckv, kr, ret_s1


def _prep_weights(l, w_in, g_cq, g_ckv, w_uq, w_uk, w_uv, g_ret, b_ret, w_out, ln1_g, ln1_b,
                  w_gate, w_up, w_down, ln2_g, ln2_b):
    return {
        "w_in": w_in[l].astype(BF16),
        "g_cq": g_cq[l].reshape(1, Q_LORA),
        "g_ckv": g_ckv[l].reshape(1, KV_LORA),
        "wqn": w_uq[l][:, :, :MLA_NOPE].reshape(Q_LORA, MLA_HEADS * MLA_NOPE).astype(BF16),
        "wqr": w_uq[l][:, :, MLA_NOPE:].reshape(Q_LORA, MLA_HEADS * MLA_ROPE).astype(BF16),
        "wk": w_uk[l].reshape(KV_LORA, MLA_HEADS * MLA_NOPE).astype(BF16),
        "wqt": jnp.transpose(w_uq[l].reshape(Q_LORA, MLA_HEADS * MLA_QK)).astype(BF16),
        "wvt": jnp.transpose(w_uv[l].reshape(KV_LORA, MLA_HEADS * MLA_DV)).astype(BF16),
        "wkt": jnp.transpose(w_uk[l], (1, 2, 0)).astype(BF16),
        "wvh": jnp.transpose(w_uv[l], (1, 0, 2)).astype(BF16),
        "g_ret": g_ret[l].reshape(1, RET_WIDTH),
        "b_ret": b_ret[l].reshape(1, RET_WIDTH),
        "w_out": w_out[l].astype(BF16),
        "ln1_g": ln1_g[l].reshape(1, 1, D_MODEL),
        "ln1_b": ln1_b[l].reshape(1, 1, D_MODEL),
        "w_gate": w_gate[l].astype(BF16),
        "w_up": w_up[l].astype(BF16),
        "w_down": w_down[l].astype(BF16),
        "ln2_g": ln2_g[l].reshape(1, 1, D_MODEL),
        "ln2_b": ln2_b[l].reshape(1, 1, D_MODEL),
    }


def kernel(x_prompt, x_sample, c_prompt, c_sample, cache_mla_ckv, cache_mla_krope, state_ret, w_ada, b_ada, w_in, g_cq, g_ckv, w_uq, w_uk, w_uv, g_ret, b_ret, w_out, ln1_g, ln1_b, w_gate, w_up, w_down, ln2_g, ln2_b):
    bp, tp, _ = x_prompt.shape
    bs, ts, _ = x_sample.shape
    past = cache_mla_ckv.shape[2]
    c_all = jnp.concatenate([c_prompt, c_sample], axis=0)
    yp, ys = x_prompt, x_sample
    outs = [[] for _ in range(6)]
    for l in range(w_ada.shape[0]):
        wts = _prep_weights(l, w_in, g_cq, g_ckv, w_uq, w_uk, w_uv, g_ret, b_ret, w_out, ln1_g, ln1_b,
                            w_gate, w_up, w_down, ln2_g, ln2_b)
        mod = _ada(c_all, w_ada[l], b_ada[l]).reshape(bp + bs, 6, 1, D_MODEL)
        mods_p = tuple(mod[:bp, i] for i in range(6))
        mods_s = tuple(mod[bp:, i] for i in range(6))
        s0 = jnp.zeros((bp, RET_HEADS, RET_DK, RET_DV), F32)
        yp, ckv_p, kr_p, rs_p = _layer(yp, mods_p, 0, None, None, s0, wts)
        ys, ckv_s, kr_s, rs_s = _layer(ys, mods_s, past, cache_mla_ckv[l], cache_mla_krope[l],
                                       state_ret[l], wts)
        for lst, val in zip(outs, (ckv_p, kr_p, rs_p, ckv_s, kr_s, rs_s)):
            lst.append(val)
    stacked = [v[0][None] if len(v) == 1 else jnp.stack(v, axis=0) for v in outs]
    return (yp, ys, *stacked)
```

```python
import functools
import math

import jax
import jax.numpy as jnp
import numpy as np
from jax import lax
from jax.experimental import pallas as pl
from jax.experimental.pallas import tpu as pltpu

F32 = jnp.float32
BF16 = jnp.bfloat16

D_MODEL = 2048
CHUNK = 64
RET_HEADS = 8
RET_DK = 64
RET_DV = 128
RET_QK = RET_HEADS * RET_DK
RET_WIDTH = RET_HEADS * RET_DV
MLA_HEADS = 8
MLA_NOPE = 128
MLA_ROPE = 64
MLA_DV = 128
MLA_DV_AUG = MLA_DV + 16
MLA_QK = MLA_NOPE + MLA_ROPE
MLA_WIDTH = MLA_HEADS * MLA_DV
Q_LORA = 512
KV_LORA = 512
MLA_SCALE = MLA_QK ** -0.5
D_FF = 5632
DEPTH = 1
ALPHA = (2 * DEPTH) ** 0.25
ROPE_BASE = 10000.0
EPS = 1e-5
LOG2E = math.log2(math.e)
NEG_BIG = -1e30

LANES = 128
ROW_TILE = 512
FFN_ROW_TILE = 1024
FFN_COL_TILE = 512
RET_CHUNK = 256
RET_CHUNKS_PER_STEP = 2
ATTN_Q_TILE = 1024
ATTN_K_TILE = 512
ATTN_HEADS_PER_STEP = 2
ADA_COL_TILE = 1024
VMEM_LIMIT = 56 * 1024 * 1024
FFN_VMEM_LIMIT = 60 * 1024 * 1024


def _params(sem, vmem=VMEM_LIMIT):
    return pltpu.CompilerParams(dimension_semantics=sem, vmem_limit_bytes=vmem)


def _const_spec(shape):
    zeros = (0,) * len(shape)
    return pl.BlockSpec(shape, lambda *_: zeros, pipeline_mode=pl.Buffered(1))


def _dot(a, b):
    return jnp.dot(a, b, preferred_element_type=F32)


def _dot_nt(a, b):
    return lax.dot_general(a, b, (((1,), (1,)), ((), ())), preferred_element_type=F32)


def _dot_tn(a, b):
    return lax.dot_general(a, b, (((0,), (0,)), ((), ())), preferred_element_type=F32)


def _silu(x):
    return x * jax.nn.sigmoid(x)


def _row_tiling(batch, seq, rows):
    if seq >= rows:
        assert seq % rows == 0
        return 1, rows
    nb = min(batch, rows // seq)
    assert batch % nb == 0
    return nb, seq


def _half_tiles(nb, rb):
    if nb > 1:
        return [(slice(h * nb // 2, (h + 1) * nb // 2), slice(None), nb // 2, rb) for h in range(2)]
    return [(slice(None), slice(h * rb // 2, (h + 1) * rb // 2), nb, rb // 2) for h in range(2)]


def _rope_cols(y, cos, sin, first_half):
    outs = []
    for g in range(y.shape[1] // LANES):
        yg = y[:, g * LANES:(g + 1) * LANES]
        rot = jnp.where(first_half, pltpu.roll(yg, LANES - 32, 1), pltpu.roll(yg, 32, 1))
        outs.append(yg * cos + rot * sin)
    return outs[0] if len(outs) == 1 else jnp.concatenate(outs, axis=1)


def _rope_angles(pos0, n):
    half = 32
    inv = ROPE_BASE ** (-np.arange(half, dtype=np.float64) / half)
    pos = (pos0 + np.arange(n)).astype(np.float64)
    return pos[:, None] * inv[None, :]


def _rope_tables(pos0, n):
    ang = _rope_angles(pos0, n)
    cos = np.cos(ang).astype(np.float32)
    sin = np.sin(ang).astype(np.float32)
    return np.concatenate([cos, cos, cos, cos], axis=1), np.concatenate([-sin, sin, -sin, sin], axis=1)


def _rope_tables_t(pos0, n):
    ang = _rope_angles(pos0, n)
    return (np.ascontiguousarray(np.cos(ang).T.astype(np.float32)),
            np.ascontiguousarray(np.sin(ang).T.astype(np.float32)))


def _tile_rows(t, nb):
    if nb == 1:
        return t
    return jnp.broadcast_to(t[None], (nb,) + t.shape).reshape(nb * t.shape[0], t.shape[1])


def _ada_kernel(c_ref, w_ref, b_ref, o_ref):
    c = c_ref[...]
    a = _silu(c).astype(BF16)
    o_ref[...] = _dot(a, w_ref[...].astype(BF16)) + b_ref[...]


def _ada(c, w_ada, b_ada):
    nrow = c.shape[0]
    ncol = w_ada.shape[1]
    tn = ADA_COL_TILE
    return pl.pallas_call(
        _ada_kernel,
        grid=(ncol // tn,),
        in_specs=[_const_spec((nrow, D_MODEL)),
                  pl.BlockSpec((D_MODEL, tn), lambda j: (0, j)),
                  pl.BlockSpec((1, tn), lambda j: (0, j))],
        out_specs=pl.BlockSpec((nrow, tn), lambda j: (0, j)),
        out_shape=jax.ShapeDtypeStruct((nrow, ncol), F32),
        compiler_params=_params(("arbitrary",)),
        name="ada_mod",
    )(c, w_ada, b_ada.reshape(1, ncol))


IN_W = 2 * RET_QK + 2 * RET_WIDTH + Q_LORA + KV_LORA + MLA_ROPE


def _inproj_kernel(x_ref, sc_ref, sh_ref, w_ref, gcq_ref, gckv_ref, cos_ref, sin_ref,
                   rq_ref, rk_ref, rv_ref, rg_ref, cq_ref, ckv_ref, kr_ref, *, nb, rb):
    def rmsnorm(y, g):
        ms = jnp.mean(y * y, axis=-1, keepdims=True)
        return y * lax.rsqrt(ms + EPS) * g

    for bsl, rsl, hb_, hr in _half_tiles(nb, rb):
        m = hb_ * hr
        h = x_ref[bsl, rsl, :] * (1.0 + sc_ref[bsl]) + sh_ref[bsl]
        hb = h.astype(BF16).reshape(m, D_MODEL)
        cos = _tile_rows(cos_ref[rsl, :], hb_)
        sin = _tile_rows(sin_ref[rsl, :], hb_)
        lane = lax.broadcasted_iota(jnp.int32, (m, LANES), 1)
        first_half = (lane % 64) < 32

        def proj(lo, hi):
            return _dot(hb, w_ref[:, lo:hi])

        def put(ref, y):
            ref[bsl, rsl, :] = y.reshape(hb_, hr, y.shape[-1])

        o = 0
        put(rq_ref, (_rope_cols(proj(o, o + RET_QK), cos, sin, first_half) * (RET_DK ** -0.5)).astype(BF16))
        o += RET_QK
        put(rk_ref, _rope_cols(proj(o, o + RET_QK), cos, sin, first_half).astype(BF16))
        o += RET_QK
        put(rv_ref, proj(o, o + RET_WIDTH).astype(BF16))
        o += RET_WIDTH
        put(rg_ref, proj(o, o + RET_WIDTH).astype(BF16))
        o += RET_WIDTH
        put(cq_ref, rmsnorm(proj(o, o + Q_LORA), gcq_ref[...]).astype(BF16))
        o += Q_LORA
        put(ckv_ref, rmsnorm(proj(o, o + KV_LORA), gckv_ref[...]))
        o += KV_LORA
        kr = proj(o, o + MLA_ROPE)
        kr = _rope_cols(jnp.concatenate([kr, kr], axis=1), cos, sin, first_half)
        put(kr_ref, kr[:, :MLA_ROPE])


def _inproj(x, sc, sh, w_in_b, g_cq, g_ckv, cos, sin):
    batch, seq, _ = x.shape
    nb, rb = _row_tiling(batch, seq, ROW_TILE)
    grid = (batch // nb, seq // rb)
    row = lambda w: pl.BlockSpec((nb, rb, w), lambda b, i: (b, i, 0))
    mod = pl.BlockSpec((nb, 1, D_MODEL), lambda b, i: (b, 0, 0))
    tab = pl.BlockSpec((rb, LANES), lambda b, i: (i, 0))
    sds = lambda w, dt: jax.ShapeDtypeStruct((batch, seq, w), dt)
    return pl.pallas_call(
        functools.partial(_inproj_kernel, nb=nb, rb=rb),
        grid=grid,
        in_specs=[row(D_MODEL), mod, mod, _const_spec((D_MODEL, IN_W)),
                  _const_spec((1, Q_LORA)), _const_spec((1, KV_LORA)), tab, tab],
        out_specs=[row(RET_QK), row(RET_QK), row(RET_WIDTH), row(RET_WIDTH),
                   row(Q_LORA), row(KV_LORA), row(MLA_ROPE)],
        out_shape=[sds(RET_QK, BF16), sds(RET_QK, BF16), sds(RET_WIDTH, BF16), sds(RET_WIDTH, BF16),
                   sds(Q_LORA, BF16), sds(KV_LORA, F32), sds(MLA_ROPE, F32)],
        compiler_params=_params(("parallel", "parallel")),
        name="in_proj",
    )(x, sc, sh, w_in_b, g_cq, g_ckv, cos, sin)


def _ret_tables(c):
    hh = np.arange(RET_HEADS, dtype=np.float64)
    logg = np.log1p(-np.exp2(-5.0 - hh))
    i = np.arange(c, dtype=np.float64)
    diff = i[:, None] - i[None, :]
    dmask = np.where(diff[None] >= 0.0, np.exp(np.maximum(diff, 0.0)[None] * logg[:, None, None]), 0.0)
    qdec = np.exp((i + 1.0)[:, None] * logg[None, :])
    kdec = np.exp((c - 1.0 - i)[:, None] * logg[None, :])
    cdec = np.exp(c * logg)
    qdec = np.repeat(qdec, RET_DK, axis=1)
    kdec = np.repeat(kdec, RET_DK, axis=1)
    r = np.arange(2 * RET_DK)[:, None] // RET_DK
    col = np.arange(2 * RET_DV)[None, :] // RET_DV
    pair = np.arange(RET_HEADS // 2)[:, None, None]
    sdec = np.where(r[None] == col[None], cdec[2 * pair + r[None]], 0.0)
    return tuple(a.astype(np.float32) for a in (dmask, qdec, kdec, sdec))


def _ret_kernel(rq_ref, rk_ref, rv_ref, rg_ref, s0_ref, dmask_ref, qdec_ref, kdec_ref, sdec_ref,
                gret_ref, bret_ref, o_ref, so_ref, s_scr, *, c, nc):
    n = pl.program_id(1)
    npair = RET_HEADS // 2

    @pl.when(n == 0)
    def _():
        s_scr[...] = jnp.zeros(s_scr.shape, F32)
        for p in range(npair):
            s_scr[p, 0:RET_DK, 0:RET_DV] = s0_ref[0, 2 * p]
            s_scr[p, RET_DK:2 * RET_DK, RET_DV:2 * RET_DV] = s0_ref[0, 2 * p + 1]

    for ci in range(nc):
        rows = slice(ci * c, (ci + 1) * c)
        q = rq_ref[0, rows]
        k = rk_ref[0, rows]
        qd = (q.astype(F32) * qdec_ref[...]).astype(BF16)
        kd = (k.astype(F32) * kdec_ref[...]).astype(BF16)
        lane = lax.broadcasted_iota(jnp.int32, (c, LANES), 1)
        lo = lane < RET_DK
        zero = jnp.zeros((c, LANES), BF16)
        for p in range(npair):
            qp = q[:, p * LANES:(p + 1) * LANES]
            kp = k[:, p * LANES:(p + 1) * LANES]
            vp = rv_ref[0, rows, p * 2 * RET_DV:(p + 1) * 2 * RET_DV]
            sc0 = _dot_nt(jnp.where(lo, qp, zero), kp) * dmask_ref[2 * p]
            sc1 = _dot_nt(jnp.where(lo, zero, qp), kp) * dmask_ref[2 * p + 1]
            intra0 = _dot(sc0.astype(BF16), vp[:, :RET_DV])
            intra1 = _dot(sc1.astype(BF16), vp[:, RET_DV:])
            s = s_scr[p]
            inter = _dot(qd[:, p * LANES:(p + 1) * LANES], s.astype(BF16))
            kv = _dot_tn(kd[:, p * LANES:(p + 1) * LANES], vp)
            sdec = sdec_ref[p]
            s_scr[p] = s * sdec + jnp.where(sdec > 0.0, kv, 0.0)
            for hh in range(2):
                hd = 2 * p + hh
                cols = slice(hd * RET_DV, (hd + 1) * RET_DV)
                oh = (intra0 if hh == 0 else intra1) + inter[:, hh * RET_DV:(hh + 1) * RET_DV]
                mu = jnp.mean(oh, axis=-1, keepdims=True)
                d = oh - mu
                var = jnp.mean(d * d, axis=-1, keepdims=True)
                y = d * lax.rsqrt(var + EPS) * gret_ref[:, cols] + bret_ref[:, cols]
                gate = rg_ref[0, rows, cols].astype(F32)
                o_ref[0, rows, cols] = (_silu(gate) * y).astype(BF16)

    @pl.when(n == pl.num_programs(1) - 1)
    def _():
        for p in range(npair):
            so_ref[0, 2 * p] = s_scr[p, 0:RET_DK, 0:RET_DV]
            so_ref[0, 2 * p + 1] = s_scr[p, RET_DK:2 * RET_DK, RET_DV:2 * RET_DV]


def _retention(rq, rk, rv, rg, s0, g_ret, b_ret):
    batch, seq, _ = rq.shape
    c = min(RET_CHUNK, seq)
    nc = min(RET_CHUNKS_PER_STEP, seq // c)
    assert seq % (c * nc) == 0
    dmask, qdec, kdec, sdec = _ret_tables(c)
    row = lambda w: pl.BlockSpec((1, c * nc, w), lambda b, n: (b, n, 0))
    st = pl.BlockSpec((1, RET_HEADS, RET_DK, RET_DV), lambda b, n: (b, 0, 0, 0))
    return pl.pallas_call(
        functools.partial(_ret_kernel, c=c, nc=nc),
        grid=(batch, seq // (c * nc)),
        in_specs=[row(RET_QK), row(RET_QK), row(RET_WIDTH), row(RET_WIDTH), st,
                  _const_spec(dmask.shape), _const_spec(qdec.shape), _const_spec(kdec.shape),
                  _const_spec(sdec.shape), _const_spec((1, RET_WIDTH)), _const_spec((1, RET_WIDTH))],
        out_specs=[row(RET_WIDTH), st],
        out_shape=[jax.ShapeDtypeStruct((batch, seq, RET_WIDTH), BF16),
                   jax.ShapeDtypeStruct((batch, RET_HEADS, RET_DK, RET_DV), F32)],
        scratch_shapes=[pltpu.VMEM((RET_HEADS // 2, 2 * RET_DK, 2 * RET_DV), F32)],
        compiler_params=_params(("parallel", "arbitrary")),
        name="retention",
    )(rq, rk, rv, rg, s0, dmask, qdec, kdec, sdec, g_ret, b_ret)


def _upproj_kernel(cq_ref, ckv_ref, kr_ref, cos_ref, sin_ref, wqt_ref, wk_ref, wvt_ref,
                   qt_ref, k_ref, vt_ref):
    cq = cq_ref[0]
    ckv = ckv_ref[0].astype(BF16)
    krb = kr_ref[0].astype(BF16)
    cos = cos_ref[...]
    sin = sin_ref[...]
    half = MLA_ROPE // 2
    qt = _dot_nt(wqt_ref[...], cq) * (MLA_SCALE * LOG2E)
    kn = _dot(ckv, wk_ref[...]).astype(BF16)
    vt = _dot_nt(wvt_ref[...], ckv).astype(BF16)
    for hd in range(MLA_HEADS):
        r0 = hd * MLA_QK
        x1 = qt[r0 + MLA_NOPE:r0 + MLA_NOPE + half]
        x2 = qt[r0 + MLA_NOPE + half:r0 + MLA_QK]
        qt_ref[0, hd, 0:MLA_NOPE] = qt[r0:r0 + MLA_NOPE].astype(BF16)
        qt_ref[0, hd, MLA_NOPE:MLA_NOPE + half] = (x1 * cos - x2 * sin).astype(BF16)
        qt_ref[0, hd, MLA_NOPE + half:MLA_QK] = (x2 * cos + x1 * sin).astype(BF16)
        k_ref[0, hd, :, 0:MLA_NOPE] = kn[:, hd * MLA_NOPE:(hd + 1) * MLA_NOPE]
        k_ref[0, hd, :, MLA_NOPE:MLA_QK] = krb
        vt_ref[0, hd, 0:MLA_DV] = vt[hd * MLA_DV:(hd + 1) * MLA_DV]
        vt_ref[0, hd, MLA_DV:MLA_DV_AUG] = jnp.ones((MLA_DV_AUG - MLA_DV, vt.shape[1]), BF16)


def _upproj(cq, ckv, kr, cos_t, sin_t, wqt, wk, wvt):
    batch, seq, _ = cq.shape
    rb = min(ROW_TILE, seq)
    assert seq % rb == 0
    row = lambda w: pl.BlockSpec((1, rb, w), lambda b, i: (b, i, 0))
    tab = pl.BlockSpec((MLA_ROPE // 2, rb), lambda b, i: (0, i))
    head_t = lambda w: pl.BlockSpec((1, MLA_HEADS, w, rb), lambda b, i: (b, 0, 0, i))
    return pl.pallas_call(
        _upproj_kernel,
        grid=(batch, seq // rb),
        in_specs=[row(Q_LORA), row(KV_LORA), row(MLA_ROPE), tab, tab,
                  _const_spec(wqt.shape), _const_spec(wk.shape), _const_spec(wvt.shape)],
        out_specs=[head_t(MLA_QK),
                   pl.BlockSpec((1, MLA_HEADS, rb, MLA_QK), lambda b, i: (b, 0, i, 0)),
                   head_t(MLA_DV_AUG)],
        out_shape=[jax.ShapeDtypeStruct((batch, MLA_HEADS, MLA_QK, seq), BF16),
                   jax.ShapeDtypeStruct((batch, MLA_HEADS, seq, MLA_QK), BF16),
                   jax.ShapeDtypeStruct((batch, MLA_HEADS, MLA_DV_AUG, seq), BF16)],
        compiler_params=_params(("parallel", "parallel")),
        name="mla_up_proj",
    )(cq, ckv, kr, cos_t, sin_t, wqt, wk, wvt)


def _flash_kernel(qt_ref, k_ref, vt_ref, o_ref, m_scr, acc_scr, sa_scr, sb_scr, ma_scr, mb_scr, *, tq, tk):
    i = pl.program_id(2)
    ratio = tq // tk
    m_scr[...] = jnp.full(m_scr.shape, -jnp.inf, F32)
    acc_scr[...] = jnp.zeros(acc_scr.shape, F32)

    heads = range(ATTN_HEADS_PER_STEP)

    def scores(j, hd, c0=0):
        start = pl.multiple_of(j * tk, tk)
        return _dot(k_ref[0, hd, pl.ds(start, tk), :], qt_ref[0, hd, :, c0:])

    def update(buf, j, hd, diag=None):
        c0 = 0 if diag is None else diag * tk
        start = pl.multiple_of(j * tk, tk)
        vtb = vt_ref[0, hd, :, pl.ds(start, tk)]
        s = buf[0][hd, :, c0:]
        if diag is None:
            smax = buf[1][hd]
        else:
            kc = (c0 + lax.broadcasted_iota(jnp.int32, (tk, tq - c0), 0)) // CHUNK
            qc = (c0 + lax.broadcasted_iota(jnp.int32, (tk, tq - c0), 1)) // CHUNK
            s = jnp.where(kc <= qc, s, NEG_BIG)
            smax = jnp.max(s, axis=0, keepdims=True)
        m_prev = m_scr[hd, :, c0:]
        m_new = jnp.maximum(m_prev, smax)
        alpha = jnp.exp2(m_prev - m_new)
        p = jnp.exp2(s - m_new)
        acc_scr[hd, :, c0:] = alpha * acc_scr[hd, :, c0:] + _dot(vtb, p.astype(BF16))
        m_scr[hd, :, c0:] = m_new

    def produce(buf, j, hd, diag=None):
        c0 = 0 if diag is None else diag * tk
        s = scores(j, hd, c0)
        buf[0][hd, :, c0:] = s
        if diag is None:
            buf[1][hd] = jnp.max(s, axis=0, keepdims=True)

    def stage(cur, nxt, j, diag=None, diag_next=None):
        for hd in heads:
            produce(nxt, j + 1, hd, diag_next)
            update(cur, j, hd, diag)

    buf_a = (sa_scr, ma_scr)
    buf_b = (sb_scr, mb_scr)
    for hd in heads:
        produce(buf_a, 0, hd)

    def body(jj, carry):
        stage(buf_a, buf_b, 2 * jj)
        stage(buf_b, buf_a, 2 * jj + 1)
        return carry

    nfull = i * ratio
    lax.fori_loop(0, i * (ratio // 2), body, 0)
    bufs = (buf_a, buf_b)
    for d in range(ratio - 1):
        stage(bufs[d % 2], bufs[(d + 1) % 2], nfull + d, d, d + 1)
    for hd in heads:
        update(bufs[(ratio - 1) % 2], nfull + ratio - 1, hd, ratio - 1)

    for hd in heads:
        out_t = acc_scr[hd, 0:MLA_DV] / acc_scr[hd, MLA_DV:MLA_DV + 1]
        o_ref[0, :, hd * MLA_DV:(hd + 1) * MLA_DV] = jnp.transpose(out_t).astype(BF16)


def _flash(qt, k, vt):
    batch, heads, seq, _ = k.shape
    tq = min(ATTN_Q_TILE, seq)
    tk = min(ATTN_K_TILE, seq // 2)
    g = ATTN_HEADS_PER_STEP
    assert seq % tq == 0 and tq % (2 * tk) == 0 and tk % CHUNK == 0 and heads % g == 0
    return pl.pallas_call(
        functools.partial(_flash_kernel, tq=tq, tk=tk),
        grid=(batch, heads // g, seq // tq),
        in_specs=[pl.BlockSpec((1, g, MLA_QK, tq), lambda b, h, i: (b, h, 0, i)),
                  pl.BlockSpec((1, g, seq, MLA_QK), lambda b, h, i: (b, h, 0, 0)),
                  pl.BlockSpec((1, g, MLA_DV_AUG, seq), lambda b, h, i: (b, h, 0, 0))],
        out_specs=pl.BlockSpec((1, tq, g * MLA_DV), lambda b, h, i: (b, i, h)),
        out_shape=jax.ShapeDtypeStruct((batch, seq, MLA_WIDTH), BF16),
        scratch_shapes=[pltpu.VMEM((g, 1, tq), F32),
                        pltpu.VMEM((g, MLA_DV_AUG, tq), F32), pltpu.VMEM((g, tk, tq), F32),
                        pltpu.VMEM((g, tk, tq), F32), pltpu.VMEM((g, 1, tq), F32), pltpu.VMEM((g, 1, tq), F32)],
        compiler_params=_params(("parallel", "parallel", "arbitrary")),
        name="mla_flash",
    )(qt, k, vt)


def _cached_attn_kernel(cq_ref, ckvn_ref, krn_ref, ckvp_ref, krp_ref, cos_ref, sin_ref,
                        wqn_ref, wqr_ref, wkt_ref, wv_ref, o_ref, *, tq, past):
    cq = cq_ref[0]
    lane = lax.broadcasted_iota(jnp.int32, (tq, LANES), 1)
    first_half = (lane % 64) < 32
    qscale = MLA_SCALE * LOG2E
    qn = (_dot(cq, wqn_ref[...]) * qscale).astype(BF16)
    qr = (_rope_cols(_dot(cq, wqr_ref[...]), cos_ref[...], sin_ref[...], first_half) * qscale).astype(BF16)
    qlat = jnp.concatenate(
        [_dot(qn[:, hd * MLA_NOPE:(hd + 1) * MLA_NOPE], wkt_ref[hd]) for hd in range(MLA_HEADS)],
        axis=0).astype(BF16)
    qrr = jnp.concatenate([qr[:, hd * MLA_ROPE:(hd + 1) * MLA_ROPE] for hd in range(MLA_HEADS)], axis=0)
    ckv_p = ckvp_ref[0].astype(BF16)
    kr_p = krp_ref[0].astype(BF16)
    ckv_n = ckvn_ref[0].astype(BF16)
    kr_n = krn_ref[0].astype(BF16)
    s_p = _dot_nt(qlat, ckv_p) + _dot_nt(qrr, kr_p)
    s_n = _dot_nt(qlat, ckv_n) + _dot_nt(qrr, kr_n)
    rows = MLA_HEADS * tq
    qpos = past + lax.broadcasted_iota(jnp.int32, (rows, tq), 0) % tq
    kpos = past + lax.broadcasted_iota(jnp.int32, (rows, tq), 1)
    s_n = jnp.where(kpos // CHUNK <= qpos // CHUNK, s_n, NEG_BIG)
    m = jnp.maximum(jnp.max(s_p, axis=1, keepdims=True), jnp.max(s_n, axis=1, keepdims=True))
    p_p = jnp.exp2(s_p - m)
    p_n = jnp.exp2(s_n - m)
    l = jnp.sum(p_p, axis=1, keepdims=True) + jnp.sum(p_n, axis=1, keepdims=True)
    olat = (_dot(p_p.astype(BF16), ckv_p) + _dot(p_n.astype(BF16), ckv_n)) / l
    olat = olat.astype(BF16)
    for hd in range(MLA_HEADS):
        o_ref[0, :, hd * MLA_DV:(hd + 1) * MLA_DV] = _dot(olat[hd * tq:(hd + 1) * tq], wv_ref[hd]).astype(BF16)


def _cached_attn(cq, ckv_new, kr_new, ckv_past, kr_past, cos, sin, wqn, wqr, wkt, wvh):
    batch, tq, _ = cq.shape
    past = ckv_past.shape[1]
    assert past % CHUNK == 0
    one = lambda n, w: pl.BlockSpec((1, n, w), lambda b: (b, 0, 0))
    return pl.pallas_call(
        functools.partial(_cached_attn_kernel, tq=tq, past=past),
        grid=(batch,),
        in_specs=[one(tq, Q_LORA), one(tq, KV_LORA), one(tq, MLA_ROPE), one(past, KV_LORA), one(past, MLA_ROPE),
                  _const_spec(cos.shape), _const_spec(sin.shape), _const_spec(wqn.shape), _const_spec(wqr.shape),
                  _const_spec(wkt.shape), _const_spec(wvh.shape)],
        out_specs=one(tq, MLA_WIDTH),
        out_shape=jax.ShapeDtypeStruct((batch, tq, MLA_WIDTH), BF16),
        compiler_params=_params(("parallel",)),
        name="mla_cached",
    )(cq, ckv_new, kr_new, ckv_past, kr_past, cos, sin, wqn, wqr, wkt, wvh)


def _layernorm(y, g, b):
    mu = jnp.mean(y, axis=-1, keepdims=True)
    d = y - mu
    var = jnp.mean(d * d, axis=-1, keepdims=True)
    return d * lax.rsqrt(var + EPS) * g + b


def _outproj_kernel(oret_ref, omla_ref, x_ref, g1_ref, w_ref, lng_ref, lnb_ref, x1_ref, *, nb, rb):
    for bsl, rsl, hb, hr in _half_tiles(nb, rb):
        a = oret_ref[bsl, rsl, :].reshape(hb * hr, RET_WIDTH)
        b = omla_ref[bsl, rsl, :].reshape(hb * hr, MLA_WIDTH)
        mix = _dot(a, w_ref[0:RET_WIDTH, :]) + _dot(b, w_ref[RET_WIDTH:RET_WIDTH + MLA_WIDTH, :])
        y = ALPHA * x_ref[bsl, rsl, :] + g1_ref[bsl] * mix.reshape(hb, hr, D_MODEL)
        x1_ref[bsl, rsl, :] = _layernorm(y, lng_ref[...], lnb_ref[...])


def _outproj(o_ret, o_mla, x, g1, w_out_b, ln_g, ln_b):
    batch, seq, _ = x.shape
    nb, rb = _row_tiling(batch, seq, ROW_TILE)
    row = lambda w: pl.BlockSpec((nb, rb, w), lambda b, i: (b, i, 0))
    mod = pl.BlockSpec((nb, 1, D_MODEL), lambda b, i: (b, 0, 0))
    return pl.pallas_call(
        functools.partial(_outproj_kernel, nb=nb, rb=rb),
        grid=(batch // nb, seq // rb),
        in_specs=[row(RET_WIDTH), row(MLA_WIDTH), row(D_MODEL), mod, _const_spec(w_out_b.shape),
                  _const_spec((1, 1, D_MODEL)), _const_spec((1, 1, D_MODEL))],
        out_specs=row(D_MODEL),
        out_shape=jax.ShapeDtypeStruct((batch, seq, D_MODEL), F32),
        compiler_params=_params(("parallel", "parallel")),
        name="out_proj_ln1",
    )(o_ret, o_mla, x, g1, w_out_b, ln_g, ln_b)


def _ffn_kernel(x1_ref, sc_ref, sh_ref, g2_ref, wg_ref, wu_ref, wd_ref, lng_ref, lnb_ref, o_ref, h_scr,
                *, nb, rb):
    m = nb * rb
    j = pl.program_id(2)

    @pl.when(j == 0)
    def _():
        x1 = x1_ref[...]
        h_scr[...] = (x1 * (1.0 + sc_ref[...]) + sh_ref[...]).astype(BF16).reshape(m, D_MODEL)
        o_ref[...] = ALPHA * x1

    h = h_scr[...]
    act = (_silu(_dot(h, wg_ref[...])) * _dot(h, wu_ref[...])).astype(BF16)
    for n in range(D_MODEL // FFN_COL_TILE):
        cols = slice(n * FFN_COL_TILE, (n + 1) * FFN_COL_TILE)
        part = _dot(act, wd_ref[:, cols]).reshape(nb, rb, FFN_COL_TILE)
        o_ref[:, :, cols] += part * g2_ref[:, :, cols]

    @pl.when(j == pl.num_programs(2) - 1)
    def _():
        o_ref[...] = _layernorm(o_ref[...], lng_ref[...], lnb_ref[...])


def _ffn(x1, sc, sh, g2, wg, wu, wd, ln_g, ln_b):
    batch, seq, _ = x1.shape
    nb, rb = _row_tiling(batch, seq, FFN_ROW_TILE)
    tf = FFN_COL_TILE
    assert D_FF % tf == 0
    row = pl.BlockSpec((nb, rb, D_MODEL), lambda b, i, j: (b, i, 0))
    mod = pl.BlockSpec((nb, 1, D_MODEL), lambda b, i, j: (b, 0, 0))
    return pl.pallas_call(
        functools.partial(_ffn_kernel, nb=nb, rb=rb),
        grid=(batch // nb, seq // rb, D_FF // tf),
        in_specs=[row, mod, mod, mod,
                  pl.BlockSpec((D_MODEL, tf), lambda b, i, j: (0, j)),
                  pl.BlockSpec((D_MODEL, tf), lambda b, i, j: (0, j)),
                  pl.BlockSpec((tf, D_MODEL), lambda b, i, j: (j, 0)),
                  _const_spec((1, 1, D_MODEL)), _const_spec((1, 1, D_MODEL))],
        out_specs=row,
        out_shape=jax.ShapeDtypeStruct((batch, seq, D_MODEL), F32),
        scratch_shapes=[pltpu.VMEM((nb * rb, D_MODEL), BF16)],
        compiler_params=_params(("parallel", "parallel", "arbitrary"), FFN_VMEM_LIMIT),
        name="swiglu_ln2",
    )(x1, sc, sh, g2, wg, wu, wd, ln_g, ln_b)


def _layer(x, mods, pos0, ckv_past, kr_past, ret_s0, wts):
    sh1, sc1, g1, sh2, sc2, g2 = mods
    seq = x.shape[1]
    cos, sin = _rope_tables(pos0, seq)
    rq, rk, rv, rg, cq, ckv, kr = _inproj(x, sc1, sh1, wts["w_in"], wts["g_cq"], wts["g_ckv"], cos, sin)
    o_ret, ret_s1 = _retention(rq, rk, rv, rg, ret_s0, wts["g_ret"], wts["b_ret"])
    if ckv_past is None:
        cos_t, sin_t = _rope_tables_t(pos0, seq)
        qt, k, vt = _upproj(cq, ckv, kr, cos_t, sin_t, wts["wqt"], wts["wk"], wts["wvt"])
        o_mla = _flash(qt, k, vt)
    else:
        o_mla = _cached_attn(cq, ckv, kr, ckv_past, kr_past, cos, sin,
                             wts["wqn"], wts["wqr"], wts["wkt"], wts["wvh"])
    x1 = _outproj(o_ret, o_mla, x, g1, wts["w_out"], wts["ln1_g"], wts["ln1_b"])
    y = _ffn(x1, sc2, sh2, g2, wts["w_gate"], wts["w_up"], wts["w_down"], wts["ln2_g"], wts["ln2_b"])
    return y, ckv, kr, ret_s1


def _prep_weights(l, w_in, g_cq, g_ckv, w_uq, w_uk, w_uv, g_ret, b_ret, w_out, ln1_g, ln1_b,
                  w_gate, w_up, w_down, ln2_g, ln2_b):
    return {
        "w_in": w_in[l].astype(BF16),
        "g_cq": g_cq[l].reshape(1, Q_LORA),
        "g_ckv": g_ckv[l].reshape(1, KV_LORA),
        "wqn": w_uq[l][:, :, :MLA_NOPE].reshape(Q_LORA, MLA_HEADS * MLA_NOPE).astype(BF16),
        "wqr": w_uq[l][:, :, MLA_NOPE:].reshape(Q_LORA, MLA_HEADS * MLA_ROPE).astype(BF16),
        "wk": w_uk[l].reshape(KV_LORA, MLA_HEADS * MLA_NOPE).astype(BF16),
        "wqt": jnp.transpose(w_uq[l].reshape(Q_LORA, MLA_HEADS * MLA_QK)).astype(BF16),
        "wvt": jnp.transpose(w_uv[l].reshape(KV_LORA, MLA_HEADS * MLA_DV)).astype(BF16),
        "wkt": jnp.transpose(w_uk[l], (1, 2, 0)).astype(BF16),
        "wvh": jnp.transpose(w_uv[l], (1, 0, 2)).astype(BF16),
        "g_ret": g_ret[l].reshape(1, RET_WIDTH),
        "b_ret": b_ret[l].reshape(1, RET_WIDTH),
        "w_out": w_out[l].astype(BF16),
        "ln1_g": ln1_g[l].reshape(1, 1, D_MODEL),
        "ln1_b": ln1_b[l].reshape(1, 1, D_MODEL),
        "w_gate": w_gate[l].astype(BF16),
        "w_up": w_up[l].astype(BF16),
        "w_down": w_down[l].astype(BF16),
        "ln2_g": ln2_g[l].reshape(1, 1, D_MODEL),
        "ln2_b": ln2_b[l].reshape(1, 1, D_MODEL),
    }


def kernel(x_prompt, x_sample, c_prompt, c_sample, cache_mla_ckv, cache_mla_krope, state_ret, w_ada, b_ada, w_in, g_cq, g_ckv, w_uq, w_uk, w_uv, g_ret, b_ret, w_out, ln1_g, ln1_b, w_gate, w_up, w_down, ln2_g, ln2_b):
    bp, tp, _ = x_prompt.shape
    bs, ts, _ = x_sample.shape
    past = cache_mla_ckv.shape[2]
    c_all = jnp.concatenate([c_prompt, c_sample], axis=0)
    yp, ys = x_prompt, x_sample
    outs = [[] for _ in range(6)]
    for l in range(w_ada.shape[0]):
        wts = _prep_weights(l, w_in, g_cq, g_ckv, w_uq, w_uk, w_uv, g_ret, b_ret, w_out, ln1_g, ln1_b,
                            w_gate, w_up, w_down, ln2_g, ln2_b)
        mod = _ada(c_all, w_ada[l], b_ada[l]).reshape(bp + bs, 6, 1, D_MODEL)
        mods_p = tuple(mod[:bp, i] for i in range(6))
        mods_s = tuple(mod[bp:, i] for i in range(6))
        s0 = jnp.zeros((bp, RET_HEADS, RET_DK, RET_DV), F32)
        yp, ckv_p, kr_p, rs_p = _layer(yp, mods_p, 0, None, None, s0, wts)
        ys, ckv_s, kr_s, rs_s = _layer(ys, mods_s, past, cache_mla_ckv[l], cache_mla_krope[l],
                                       state_ret[l], wts)
        for lst, val in zip(outs, (ckv_p, kr_p, rs_p, ckv_s, kr_s, rs_s)):
            lst.append(val)
    stacked = [v[0][None] if len(v) == 1 else jnp.stack(v, axis=0) for v in outs]
    return (yp, ys, *stacked)
```

```python
import functools
import math

import jax
import jax.numpy as jnp
import numpy as np
from jax import lax
from jax.experimental import pallas as pl
from jax.experimental.pallas import tpu as pltpu

F32 = jnp.float32
BF16 = jnp.bfloat16

D_MODEL = 2048
CHUNK = 64
RET_HEADS = 8
RET_DK = 64
RET_DV = 128
RET_QK = RET_HEADS * RET_DK
RET_WIDTH = RET_HEADS * RET_DV
MLA_HEADS = 8
MLA_NOPE = 128
MLA_ROPE = 64
MLA_DV = 128
MLA_DV_AUG = MLA_DV + 16
MLA_QK = MLA_NOPE + MLA_ROPE
MLA_WIDTH = MLA_HEADS * MLA_DV
Q_LORA = 512
KV_LORA = 512
MLA_SCALE = MLA_QK ** -0.5
D_FF = 5632
DEPTH = 1
ALPHA = (2 * DEPTH) ** 0.25
ROPE_BASE = 10000.0
EPS = 1e-5
LOG2E = math.log2(math.e)
NEG_BIG = -1e30

LANES = 128
ROW_TILE = 512
FFN_ROW_TILE = 1024
FFN_COL_TILE = 512
RET_CHUNK = 256
RET_CHUNKS_PER_STEP = 2
ATTN_Q_TILE = 1024
ATTN_K_TILE = 512
ATTN_HEADS_PER_STEP = 2
ADA_COL_TILE = 1024
VMEM_LIMIT = 56 * 1024 * 1024
FFN_VMEM_LIMIT = 60 * 1024 * 1024


def _params(sem, vmem=VMEM_LIMIT):
    return pltpu.CompilerParams(dimension_semantics=sem, vmem_limit_bytes=vmem)


def _const_spec(shape):
    zeros = (0,) * len(shape)
    return pl.BlockSpec(shape, lambda *_: zeros, pipeline_mode=pl.Buffered(1))


def _dot(a, b):
    return jnp.dot(a, b, preferred_element_type=F32)


def _dot_nt(a, b):
    return lax.dot_general(a, b, (((1,), (1,)), ((), ())), preferred_element_type=F32)


def _dot_tn(a, b):
    return lax.dot_general(a, b, (((0,), (0,)), ((), ())), preferred_element_type=F32)


def _silu(x):
    return x * jax.nn.sigmoid(x)


def _row_tiling(batch, seq, rows):
    if seq >= rows:
        assert seq % rows == 0
        return 1, rows
    nb = min(batch, rows // seq)
    assert batch % nb == 0
    return nb, seq


def _half_tiles(nb, rb):
    if nb > 1:
        return [(slice(h * nb // 2, (h + 1) * nb // 2), slice(None), nb // 2, rb) for h in range(2)]
    return [(slice(None), slice(h * rb // 2, (h + 1) * rb // 2), nb, rb // 2) for h in range(2)]


def _rope_cols(y, cos, sin, first_half):
    outs = []
    for g in range(y.shape[1] // LANES):
        yg = y[:, g * LANES:(g + 1) * LANES]
        rot = jnp.where(first_half, pltpu.roll(yg, LANES - 32, 1), pltpu.roll(yg, 32, 1))
        outs.append(yg * cos + rot * sin)
    return outs[0] if len(outs) == 1 else jnp.concatenate(outs, axis=1)


def _rope_angles(pos0, n):
    half = 32
    inv = ROPE_BASE ** (-np.arange(half, dtype=np.float64) / half)
    pos = (pos0 + np.arange(n)).astype(np.float64)
    return pos[:, None] * inv[None, :]


def _rope_tables(pos0, n):
    ang = _rope_angles(pos0, n)
    cos = np.cos(ang).astype(np.float32)
    sin = np.sin(ang).astype(np.float32)
    return np.concatenate([cos, cos, cos, cos], axis=1), np.concatenate([-sin, sin, -sin, sin], axis=1)


def _rope_tables_t(pos0, n):
    ang = _rope_angles(pos0, n)
    return (np.ascontiguousarray(np.cos(ang).T.astype(np.float32)),
            np.ascontiguousarray(np.sin(ang).T.astype(np.float32)))


def _tile_rows(t, nb):
    if nb == 1:
        return t
    return jnp.broadcast_to(t[None], (nb,) + t.shape).reshape(nb * t.shape[0], t.shape[1])


def _ada_kernel(c_ref, w_ref, b_ref, o_ref):
    c = c_ref[...]
    a = _silu(c).astype(BF16)
    o_ref[...] = _dot(a, w_ref[...].astype(BF16)) + b_ref[...]


def _ada(c, w_ada, b_ada):
    nrow = c.shape[0]
    ncol = w_ada.shape[1]
    tn = ADA_COL_TILE
    return pl.pallas_call(
        _ada_kernel,
        grid=(ncol // tn,),
        in_specs=[_const_spec((nrow, D_MODEL)),
                  pl.BlockSpec((D_MODEL, tn), lambda j: (0, j)),
                  pl.BlockSpec((1, tn), lambda j: (0, j))],
        out_specs=pl.BlockSpec((nrow, tn), lambda j: (0, j)),
        out_shape=jax.ShapeDtypeStruct((nrow, ncol), F32),
        compiler_params=_params(("arbitrary",)),
        name="ada_mod",
    )(c, w_ada, b_ada.reshape(1, ncol))


IN_W = 2 * RET_QK + 2 * RET_WIDTH + Q_LORA + KV_LORA + MLA_ROPE


def _inproj_kernel(x_ref, sc_ref, sh_ref, w_ref, gcq_ref, gckv_ref, cos_ref, sin_ref, *rest, nb, rb, ncast):
    cast_in = rest[:ncast]
    rq_ref, rk_ref, rv_ref, rg_ref, cq_ref, ckv_ref, kr_ref = rest[ncast:ncast + 7]
    cast_out = rest[ncast + 7:]
    for src, dst in zip(cast_in, cast_out):
        dst[...] = src[...].astype(BF16)

    def rmsnorm(y, g):
        ms = jnp.mean(y * y, axis=-1, keepdims=True)
        return y * lax.rsqrt(ms + EPS) * g

    for bsl, rsl, hb_, hr in _half_tiles(nb, rb):
        m = hb_ * hr
        h = x_ref[bsl, rsl, :] * (1.0 + sc_ref[bsl]) + sh_ref[bsl]
        hb = h.astype(BF16).reshape(m, D_MODEL)
        cos = _tile_rows(cos_ref[rsl, :], hb_)
        sin = _tile_rows(sin_ref[rsl, :], hb_)
        lane = lax.broadcasted_iota(jnp.int32, (m, LANES), 1)
        first_half = (lane % 64) < 32

        def proj(lo, hi):
            return _dot(hb, w_ref[:, lo:hi])

        def put(ref, y):
            ref[bsl, rsl, :] = y.reshape(hb_, hr, y.shape[-1])

        o = 0
        put(rq_ref, (_rope_cols(proj(o, o + RET_QK), cos, sin, first_half) * (RET_DK ** -0.5)).astype(BF16))
        o += RET_QK
        put(rk_ref, _rope_cols(proj(o, o + RET_QK), cos, sin, first_half).astype(BF16))
        o += RET_QK
        put(rv_ref, proj(o, o + RET_WIDTH).astype(BF16))
        o += RET_WIDTH
        put(rg_ref, proj(o, o + RET_WIDTH).astype(BF16))
        o += RET_WIDTH
        put(cq_ref, rmsnorm(proj(o, o + Q_LORA), gcq_ref[...]).astype(BF16))
        o += Q_LORA
        put(ckv_ref, rmsnorm(proj(o, o + KV_LORA), gckv_ref[...]))
        o += KV_LORA
        kr = proj(o, o + MLA_ROPE)
        kr = _rope_cols(jnp.concatenate([kr, kr], axis=1), cos, sin, first_half)
        put(kr_ref, kr[:, :MLA_ROPE])


def _inproj(x, sc, sh, w_in_b, g_cq, g_ckv, cos, sin, cast=()):
    batch, seq, _ = x.shape
    nb, rb = _row_tiling(batch, seq, ROW_TILE)
    grid = (batch // nb, seq // rb)
    nsteps = grid[0] * grid[1]
    cast_spec = lambda w: pl.BlockSpec((w.shape[0] // nsteps, w.shape[1]), lambda b, i: (b * grid[1] + i, 0))
    row = lambda w: pl.BlockSpec((nb, rb, w), lambda b, i: (b, i, 0))
    mod = pl.BlockSpec((nb, 1, D_MODEL), lambda b, i: (b, 0, 0))
    tab = pl.BlockSpec((rb, LANES), lambda b, i: (i, 0))
    sds = lambda w, dt: jax.ShapeDtypeStruct((batch, seq, w), dt)
    return pl.pallas_call(
        functools.partial(_inproj_kernel, nb=nb, rb=rb, ncast=len(cast)),
        grid=grid,
        in_specs=[row(D_MODEL), mod, mod, _const_spec((D_MODEL, IN_W)),
                  _const_spec((1, Q_LORA)), _const_spec((1, KV_LORA)), tab, tab] + [cast_spec(w) for w in cast],
        out_specs=[row(RET_QK), row(RET_QK), row(RET_WIDTH), row(RET_WIDTH),
                   row(Q_LORA), row(KV_LORA), row(MLA_ROPE)] + [cast_spec(w) for w in cast],
        out_shape=[sds(RET_QK, BF16), sds(RET_QK, BF16), sds(RET_WIDTH, BF16), sds(RET_WIDTH, BF16),
                   sds(Q_LORA, BF16), sds(KV_LORA, F32), sds(MLA_ROPE, F32)]
                  + [jax.ShapeDtypeStruct(w.shape, BF16) for w in cast],
        compiler_params=_params(("parallel", "parallel")),
        name="in_proj",
    )(x, sc, sh, w_in_b, g_cq, g_ckv, cos, sin, *cast)


def _ret_tables(c):
    hh = np.arange(RET_HEADS, dtype=np.float64)
    logg = np.log1p(-np.exp2(-5.0 - hh))
    i = np.arange(c, dtype=np.float64)
    diff = i[:, None] - i[None, :]
    dmask = np.where(diff[None] >= 0.0, np.exp(np.maximum(diff, 0.0)[None] * logg[:, None, None]), 0.0)
    qdec = np.exp((i + 1.0)[:, None] * logg[None, :])
    kdec = np.exp((c - 1.0 - i)[:, None] * logg[None, :])
    cdec = np.exp(c * logg)
    qdec = np.repeat(qdec, RET_DK, axis=1)
    kdec = np.repeat(kdec, RET_DK, axis=1)
    r = np.arange(2 * RET_DK)[:, None] // RET_DK
    col = np.arange(2 * RET_DV)[None, :] // RET_DV
    pair = np.arange(RET_HEADS // 2)[:, None, None]
    sdec = np.where(r[None] == col[None], cdec[2 * pair + r[None]], 0.0)
    return tuple(a.astype(np.float32) for a in (dmask, qdec, kdec, sdec))


def _ret_kernel(rq_ref, rk_ref, rv_ref, rg_ref, s0_ref, dmask_ref, qdec_ref, kdec_ref, sdec_ref,
                gret_ref, bret_ref, o_ref, so_ref, s_scr, *, c, nc):
    n = pl.program_id(1)
    npair = RET_HEADS // 2

    @pl.when(n == 0)
    def _():
        s_scr[...] = jnp.zeros(s_scr.shape, F32)
        for p in range(npair):
            s_scr[p, 0:RET_DK, 0:RET_DV] = s0_ref[0, 2 * p]
            s_scr[p, RET_DK:2 * RET_DK, RET_DV:2 * RET_DV] = s0_ref[0, 2 * p + 1]

    for ci in range(nc):
        rows = slice(ci * c, (ci + 1) * c)
        q = rq_ref[0, rows]
        k = rk_ref[0, rows]
        qd = (q.astype(F32) * qdec_ref[...]).astype(BF16)
        kd = (k.astype(F32) * kdec_ref[...]).astype(BF16)
        lane = lax.broadcasted_iota(jnp.int32, (c, LANES), 1)
        lo = lane < RET_DK
        zero = jnp.zeros((c, LANES), BF16)
        for p in range(npair):
            qp = q[:, p * LANES:(p + 1) * LANES]
            kp = k[:, p * LANES:(p + 1) * LANES]
            vp = rv_ref[0, rows, p * 2 * RET_DV:(p + 1) * 2 * RET_DV]
            sc0 = _dot_nt(jnp.where(lo, qp, zero), kp) * dmask_ref[2 * p]
            sc1 = _dot_nt(jnp.where(lo, zero, qp), kp) * dmask_ref[2 * p + 1]
            intra0 = _dot(sc0.astype(BF16), vp[:, :RET_DV])
            intra1 = _dot(sc1.astype(BF16), vp[:, RET_DV:])
            s = s_scr[p]
            inter = _dot(qd[:, p * LANES:(p + 1) * LANES], s.astype(BF16))
            kv = _dot_tn(kd[:, p * LANES:(p + 1) * LANES], vp)
            sdec = sdec_ref[p]
            s_scr[p] = s * sdec + jnp.where(sdec > 0.0, kv, 0.0)
            for hh in range(2):
                hd = 2 * p + hh
                cols = slice(hd * RET_DV, (hd + 1) * RET_DV)
                oh = (intra0 if hh == 0 else intra1) + inter[:, hh * RET_DV:(hh + 1) * RET_DV]
                mu = jnp.mean(oh, axis=-1, keepdims=True)
                d = oh - mu
                var = jnp.mean(d * d, axis=-1, keepdims=True)
                y = d * lax.rsqrt(var + EPS) * gret_ref[:, cols] + bret_ref[:, cols]
                gate = rg_ref[0, rows, cols].astype(F32)
                o_ref[0, rows, cols] = (_silu(gate) * y).astype(BF16)

    @pl.when(n == pl.num_programs(1) - 1)
    def _():
        for p in range(npair):
            so_ref[0, 2 * p] = s_scr[p, 0:RET_DK, 0:RET_DV]
            so_ref[0, 2 * p + 1] = s_scr[p, RET_DK:2 * RET_DK, RET_DV:2 * RET_DV]


def _retention(rq, rk, rv, rg, s0, g_ret, b_ret):
    batch, seq, _ = rq.shape
    c = min(RET_CHUNK, seq)
    nc = min(RET_CHUNKS_PER_STEP, seq // c)
    assert seq % (c * nc) == 0
    dmask, qdec, kdec, sdec = _ret_tables(c)
    row = lambda w: pl.BlockSpec((1, c * nc, w), lambda b, n: (b, n, 0))
    st = pl.BlockSpec((1, RET_HEADS, RET_DK, RET_DV), lambda b, n: (b, 0, 0, 0))
    return pl.pallas_call(
        functools.partial(_ret_kernel, c=c, nc=nc),
        grid=(batch, seq // (c * nc)),
        in_specs=[row(RET_QK), row(RET_QK), row(RET_WIDTH), row(RET_WIDTH), st,
                  _const_spec(dmask.shape), _const_spec(qdec.shape), _const_spec(kdec.shape),
                  _const_spec(sdec.shape), _const_spec((1, RET_WIDTH)), _const_spec((1, RET_WIDTH))],
        out_specs=[row(RET_WIDTH), st],
        out_shape=[jax.ShapeDtypeStruct((batch, seq, RET_WIDTH), BF16),
                   jax.ShapeDtypeStruct((batch, RET_HEADS, RET_DK, RET_DV), F32)],
        scratch_shapes=[pltpu.VMEM((RET_HEADS // 2, 2 * RET_DK, 2 * RET_DV), F32)],
        compiler_params=_params(("parallel", "arbitrary")),
        name="retention",
    )(rq, rk, rv, rg, s0, dmask, qdec, kdec, sdec, g_ret, b_ret)


def _upproj_kernel(cq_ref, ckv_ref, kr_ref, cos_ref, sin_ref, wqt_ref, wk_ref, wvt_ref,
                   qt_ref, k_ref, vt_ref):
    cq = cq_ref[0]
    ckv = ckv_ref[0].astype(BF16)
    krb = kr_ref[0].astype(BF16)
    cos = cos_ref[...]
    sin = sin_ref[...]
    half = MLA_ROPE // 2
    qt = _dot_nt(wqt_ref[...], cq) * (MLA_SCALE * LOG2E)
    kn = _dot(ckv, wk_ref[...]).astype(BF16)
    vt = _dot_nt(wvt_ref[...], ckv).astype(BF16)
    for hd in range(MLA_HEADS):
        r0 = hd * MLA_QK
        x1 = qt[r0 + MLA_NOPE:r0 + MLA_NOPE + half]
        x2 = qt[r0 + MLA_NOPE + half:r0 + MLA_QK]
        qt_ref[0, hd, 0:MLA_NOPE] = qt[r0:r0 + MLA_NOPE].astype(BF16)
        qt_ref[0, hd, MLA_NOPE:MLA_NOPE + half] = (x1 * cos - x2 * sin).astype(BF16)
        qt_ref[0, hd, MLA_NOPE + half:MLA_QK] = (x2 * cos + x1 * sin).astype(BF16)
        k_ref[0, hd, :, 0:MLA_NOPE] = kn[:, hd * MLA_NOPE:(hd + 1) * MLA_NOPE]
        k_ref[0, hd, :, MLA_NOPE:MLA_QK] = krb
        vt_ref[0, hd, 0:MLA_DV] = vt[hd * MLA_DV:(hd + 1) * MLA_DV]
        vt_ref[0, hd, MLA_DV:MLA_DV_AUG] = jnp.ones((MLA_DV_AUG - MLA_DV, vt.shape[1]), BF16)


def _upproj(cq, ckv, kr, cos_t, sin_t, wqt, wk, wvt):
    batch, seq, _ = cq.shape
    rb = min(ROW_TILE, seq)
    assert seq % rb == 0
    row = lambda w: pl.BlockSpec((1, rb, w), lambda b, i: (b, i, 0))
    tab = pl.BlockSpec((MLA_ROPE // 2, rb), lambda b, i: (0, i))
    head_t = lambda w: pl.BlockSpec((1, MLA_HEADS, w, rb), lambda b, i: (b, 0, 0, i))
    return pl.pallas_call(
        _upproj_kernel,
        grid=(batch, seq // rb),
        in_specs=[row(Q_LORA), row(KV_LORA), row(MLA_ROPE), tab, tab,
                  _const_spec(wqt.shape), _const_spec(wk.shape), _const_spec(wvt.shape)],
        out_specs=[head_t(MLA_QK),
                   pl.BlockSpec((1, MLA_HEADS, rb, MLA_QK), lambda b, i: (b, 0, i, 0)),
                   head_t(MLA_DV_AUG)],
        out_shape=[jax.ShapeDtypeStruct((batch, MLA_HEADS, MLA_QK, seq), BF16),
                   jax.ShapeDtypeStruct((batch, MLA_HEADS, seq, MLA_QK), BF16),
                   jax.ShapeDtypeStruct((batch, MLA_HEADS, MLA_DV_AUG, seq), BF16)],
        compiler_params=_params(("parallel", "parallel")),
        name="mla_up_proj",
    )(cq, ckv, kr, cos_t, sin_t, wqt, wk, wvt)


def _flash_kernel(qt_ref, k_ref, vt_ref, o_ref, m_scr, acc_scr, sa_scr, sb_scr, ma_scr, mb_scr, *, tq, tk):
    i = pl.program_id(2)
    ratio = tq // tk
    m_scr[...] = jnp.full(m_scr.shape, -jnp.inf, F32)
    acc_scr[...] = jnp.zeros(acc_scr.shape, F32)

    heads = range(ATTN_HEADS_PER_STEP)

    def scores(j, hd, c0=0):
        start = pl.multiple_of(j * tk, tk)
        return _dot(k_ref[0, hd, pl.ds(start, tk), :], qt_ref[0, hd, :, c0:])

    def update(buf, j, hd, diag=None):
        c0 = 0 if diag is None else diag * tk
        start = pl.multiple_of(j * tk, tk)
        vtb = vt_ref[0, hd, :, pl.ds(start, tk)]
        s = buf[0][hd, :, c0:]
        if diag is None:
            smax = buf[1][hd]
        else:
            kc = (c0 + lax.broadcasted_iota(jnp.int32, (tk, tq - c0), 0)) // CHUNK
            qc = (c0 + lax.broadcasted_iota(jnp.int32, (tk, tq - c0), 1)) // CHUNK
            s = jnp.where(kc <= qc, s, NEG_BIG)
            smax = jnp.max(s, axis=0, keepdims=True)
        m_prev = m_scr[hd, :, c0:]
        m_new = jnp.maximum(m_prev, smax)
        alpha = jnp.exp2(m_prev - m_new)
        p = jnp.exp2(s - m_new)
        acc_scr[hd, :, c0:] = alpha * acc_scr[hd, :, c0:] + _dot(vtb, p.astype(BF16))
        m_scr[hd, :, c0:] = m_new

    def produce(buf, j, hd, diag=None):
        c0 = 0 if diag is None else diag * tk
        s = scores(j, hd, c0)
        buf[0][hd, :, c0:] = s
        if diag is None:
            buf[1][hd] = jnp.max(s, axis=0, keepdims=True)

    def stage(cur, nxt, j, diag=None, diag_next=None):
        for hd in heads:
            produce(nxt, j + 1, hd, diag_next)
            update(cur, j, hd, diag)

    buf_a = (sa_scr, ma_scr)
    buf_b = (sb_scr, mb_scr)
    for hd in heads:
        produce(buf_a, 0, hd)

    def body(jj, carry):
        stage(buf_a, buf_b, 2 * jj)
        stage(buf_b, buf_a, 2 * jj + 1)
        return carry

    nfull = i * ratio
    lax.fori_loop(0, i * (ratio // 2), body, 0)
    bufs = (buf_a, buf_b)
    for d in range(ratio - 1):
        stage(bufs[d % 2], bufs[(d + 1) % 2], nfull + d, d, d + 1)
    for hd in heads:
        update(bufs[(ratio - 1) % 2], nfull + ratio - 1, hd, ratio - 1)

    for hd in heads:
        out_t = acc_scr[hd, 0:MLA_DV] / acc_scr[hd, MLA_DV:MLA_DV + 1]
        o_ref[0, :, hd * MLA_DV:(hd + 1) * MLA_DV] = jnp.transpose(out_t).astype(BF16)


def _flash(qt, k, vt):
    batch, heads, seq, _ = k.shape
    tq = min(ATTN_Q_TILE, seq)
    tk = min(ATTN_K_TILE, seq // 2)
    g = ATTN_HEADS_PER_STEP
    assert seq % tq == 0 and tq % (2 * tk) == 0 and tk % CHUNK == 0 and heads % g == 0
    return pl.pallas_call(
        functools.partial(_flash_kernel, tq=tq, tk=tk),
        grid=(batch, heads // g, seq // tq),
        in_specs=[pl.BlockSpec((1, g, MLA_QK, tq), lambda b, h, i: (b, h, 0, i)),
                  pl.BlockSpec((1, g, seq, MLA_QK), lambda b, h, i: (b, h, 0, 0)),
                  pl.BlockSpec((1, g, MLA_DV_AUG, seq), lambda b, h, i: (b, h, 0, 0))],
        out_specs=pl.BlockSpec((1, tq, g * MLA_DV), lambda b, h, i: (b, i, h)),
        out_shape=jax.ShapeDtypeStruct((batch, seq, MLA_WIDTH), BF16),
        scratch_shapes=[pltpu.VMEM((g, 1, tq), F32),
                        pltpu.VMEM((g, MLA_DV_AUG, tq), F32), pltpu.VMEM((g, tk, tq), F32),
                        pltpu.VMEM((g, tk, tq), F32), pltpu.VMEM((g, 1, tq), F32), pltpu.VMEM((g, 1, tq), F32)],
        compiler_params=_params(("parallel", "parallel", "arbitrary")),
        name="mla_flash",
    )(qt, k, vt)


def _cached_attn_kernel(cq_ref, ckvn_ref, krn_ref, ckvp_ref, krp_ref, cos_ref, sin_ref,
                        wqn_ref, wqr_ref, wkt_ref, wv_ref, o_ref, *, tq, past):
    cq = cq_ref[0]
    lane = lax.broadcasted_iota(jnp.int32, (tq, LANES), 1)
    first_half = (lane % 64) < 32
    qscale = MLA_SCALE * LOG2E
    qn = (_dot(cq, wqn_ref[...]) * qscale).astype(BF16)
    qr = (_rope_cols(_dot(cq, wqr_ref[...]), cos_ref[...], sin_ref[...], first_half) * qscale).astype(BF16)
    qlat = jnp.concatenate(
        [_dot(qn[:, hd * MLA_NOPE:(hd + 1) * MLA_NOPE], wkt_ref[hd]) for hd in range(MLA_HEADS)],
        axis=0).astype(BF16)
    qrr = jnp.concatenate([qr[:, hd * MLA_ROPE:(hd + 1) * MLA_ROPE] for hd in range(MLA_HEADS)], axis=0)
    ckv_p = ckvp_ref[0].astype(BF16)
    kr_p = krp_ref[0].astype(BF16)
    ckv_n = ckvn_ref[0].astype(BF16)
    kr_n = krn_ref[0].astype(BF16)
    s_p = _dot_nt(qlat, ckv_p) + _dot_nt(qrr, kr_p)
    s_n = _dot_nt(qlat, ckv_n) + _dot_nt(qrr, kr_n)
    rows = MLA_HEADS * tq
    qpos = past + lax.broadcasted_iota(jnp.int32, (rows, tq), 0) % tq
    kpos = past + lax.broadcasted_iota(jnp.int32, (rows, tq), 1)
    s_n = jnp.where(kpos // CHUNK <= qpos // CHUNK, s_n, NEG_BIG)
    m = jnp.maximum(jnp.max(s_p, axis=1, keepdims=True), jnp.max(s_n, axis=1, keepdims=True))
    p_p = jnp.exp2(s_p - m)
    p_n = jnp.exp2(s_n - m)
    l = jnp.sum(p_p, axis=1, keepdims=True) + jnp.sum(p_n, axis=1, keepdims=True)
    olat = (_dot(p_p.astype(BF16), ckv_p) + _dot(p_n.astype(BF16), ckv_n)) / l
    olat = olat.astype(BF16)
    for hd in range(MLA_HEADS):
        o_ref[0, :, hd * MLA_DV:(hd + 1) * MLA_DV] = _dot(olat[hd * tq:(hd + 1) * tq], wv_ref[hd]).astype(BF16)


def _cached_attn(cq, ckv_new, kr_new, ckv_past, kr_past, cos, sin, wqn, wqr, wkt, wvh):
    batch, tq, _ = cq.shape
    past = ckv_past.shape[1]
    assert past % CHUNK == 0
    one = lambda n, w: pl.BlockSpec((1, n, w), lambda b: (b, 0, 0))
    return pl.pallas_call(
        functools.partial(_cached_attn_kernel, tq=tq, past=past),
        grid=(batch,),
        in_specs=[one(tq, Q_LORA), one(tq, KV_LORA), one(tq, MLA_ROPE), one(past, KV_LORA), one(past, MLA_ROPE),
                  _const_spec(cos.shape), _const_spec(sin.shape), _const_spec(wqn.shape), _const_spec(wqr.shape),
                  _const_spec(wkt.shape), _const_spec(wvh.shape)],
        out_specs=one(tq, MLA_WIDTH),
        out_shape=jax.ShapeDtypeStruct((batch, tq, MLA_WIDTH), BF16),
        compiler_params=_params(("parallel",)),
        name="mla_cached",
    )(cq, ckv_new, kr_new, ckv_past, kr_past, cos, sin, wqn, wqr, wkt, wvh)


def _layernorm(y, g, b):
    mu = jnp.mean(y, axis=-1, keepdims=True)
    d = y - mu
    var = jnp.mean(d * d, axis=-1, keepdims=True)
    return d * lax.rsqrt(var + EPS) * g + b


def _outproj_kernel(oret_ref, omla_ref, x_ref, g1_ref, w_ref, lng_ref, lnb_ref, x1_ref, *, nb, rb):
    for bsl, rsl, hb, hr in _half_tiles(nb, rb):
        a = oret_ref[bsl, rsl, :].reshape(hb * hr, RET_WIDTH)
        b = omla_ref[bsl, rsl, :].reshape(hb * hr, MLA_WIDTH)
        mix = _dot(a, w_ref[0:RET_WIDTH, :]) + _dot(b, w_ref[RET_WIDTH:RET_WIDTH + MLA_WIDTH, :])
        y = ALPHA * x_ref[bsl, rsl, :] + g1_ref[bsl] * mix.reshape(hb, hr, D_MODEL)
        x1_ref[bsl, rsl, :] = _layernorm(y, lng_ref[...], lnb_ref[...])


def _outproj(o_ret, o_mla, x, g1, w_out_b, ln_g, ln_b):
    batch, seq, _ = x.shape
    nb, rb = _row_tiling(batch, seq, ROW_TILE)
    row = lambda w: pl.BlockSpec((nb, rb, w), lambda b, i: (b, i, 0))
    mod = pl.BlockSpec((nb, 1, D_MODEL), lambda b, i: (b, 0, 0))
    return pl.pallas_call(
        functools.partial(_outproj_kernel, nb=nb, rb=rb),
        grid=(batch // nb, seq // rb),
        in_specs=[row(RET_WIDTH), row(MLA_WIDTH), row(D_MODEL), mod, _const_spec(w_out_b.shape),
                  _const_spec((1, 1, D_MODEL)), _const_spec((1, 1, D_MODEL))],
        out_specs=row(D_MODEL),
        out_shape=jax.ShapeDtypeStruct((batch, seq, D_MODEL), F32),
        compiler_params=_params(("parallel", "parallel")),
        name="out_proj_ln1",
    )(o_ret, o_mla, x, g1, w_out_b, ln_g, ln_b)


def _ffn_kernel(x1_ref, sc_ref, sh_ref, g2_ref, wg_ref, wu_ref, wd_ref, lng_ref, lnb_ref, o_ref, h_scr,
                *, nb, rb):
    m = nb * rb
    j = pl.program_id(2)

    @pl.when(j == 0)
    def _():
        x1 = x1_ref[...]
        h_scr[...] = (x1 * (1.0 + sc_ref[...]) + sh_ref[...]).astype(BF16).reshape(m, D_MODEL)
        o_ref[...] = ALPHA * x1

    h = h_scr[...]
    act = (_silu(_dot(h, wg_ref[...])) * _dot(h, wu_ref[...])).astype(BF16)
    for n in range(D_MODEL // FFN_COL_TILE):
        cols = slice(n * FFN_COL_TILE, (n + 1) * FFN_COL_TILE)
        part = _dot(act, wd_ref[:, cols]).reshape(nb, rb, FFN_COL_TILE)
        o_ref[:, :, cols] += part * g2_ref[:, :, cols]

    @pl.when(j == pl.num_programs(2) - 1)
    def _():
        o_ref[...] = _layernorm(o_ref[...], lng_ref[...], lnb_ref[...])


def _ffn(x1, sc, sh, g2, wg, wu, wd, ln_g, ln_b):
    batch, seq, _ = x1.shape
    nb, rb = _row_tiling(batch, seq, FFN_ROW_TILE)
    tf = FFN_COL_TILE
    assert D_FF % tf == 0
    row = pl.BlockSpec((nb, rb, D_MODEL), lambda b, i, j: (b, i, 0))
    mod = pl.BlockSpec((nb, 1, D_MODEL), lambda b, i, j: (b, 0, 0))
    return pl.pallas_call(
        functools.partial(_ffn_kernel, nb=nb, rb=rb),
        grid=(batch // nb, seq // rb, D_FF // tf),
        in_specs=[row, mod, mod, mod,
                  pl.BlockSpec((D_MODEL, tf), lambda b, i, j: (0, j)),
                  pl.BlockSpec((D_MODEL, tf), lambda b, i, j: (0, j)),
                  pl.BlockSpec((tf, D_MODEL), lambda b, i, j: (j, 0)),
                  _const_spec((1, 1, D_MODEL)), _const_spec((1, 1, D_MODEL))],
        out_specs=row,
        out_shape=jax.ShapeDtypeStruct((batch, seq, D_MODEL), F32),
        scratch_shapes=[pltpu.VMEM((nb * rb, D_MODEL), BF16)],
        compiler_params=_params(("parallel", "parallel", "arbitrary"), FFN_VMEM_LIMIT),
        name="swiglu_ln2",
    )(x1, sc, sh, g2, wg, wu, wd, ln_g, ln_b)


def _layer(x, mods, pos0, ckv_past, kr_past, ret_s0, wts):
    sh1, sc1, g1, sh2, sc2, g2 = mods
    seq = x.shape[1]
    cos, sin = _rope_tables(pos0, seq)
    late = wts.pop("late_f32", {})
    outs = _inproj(x, sc1, sh1, wts["w_in"], wts["g_cq"], wts["g_ckv"], cos, sin,
                   tuple(w.reshape(D_MODEL, -1) for w in late.values()))
    rq, rk, rv, rg, cq, ckv, kr = outs[:7]
    for (name, w), wb in zip(late.items(), outs[7:]):
        wts[name] = wb.reshape(w.shape)
    o_ret, ret_s1 = _retention(rq, rk, rv, rg, ret_s0, wts["g_ret"], wts["b_ret"])
    if ckv_past is None:
        cos_t, sin_t = _rope_tables_t(pos0, seq)
        qt, k, vt = _upproj(cq, ckv, kr, cos_t, sin_t, wts["wqt"], wts["wk"], wts["wvt"])
        o_mla = _flash(qt, k, vt)
    else:
        o_mla = _cached_attn(cq, ckv, kr, ckv_past, kr_past, cos, sin,
                             wts["wqn"], wts["wqr"], wts["wkt"], wts["wvh"])
    x1 = _outproj(o_ret, o_mla, x, g1, wts["w_out"], wts["ln1_g"], wts["ln1_b"])
    y = _ffn(x1, sc2, sh2, g2, wts["w_gate"], wts["w_up"], wts["w_down"], wts["ln2_g"], wts["ln2_b"])
    return y, ckv, kr, ret_s1


def _prep_weights(l, w_in, g_cq, g_ckv, w_uq, w_uk, w_uv, g_ret, b_ret, w_out, ln1_g, ln1_b,
                  w_gate, w_up, w_down, ln2_g, ln2_b):
    return {
        "w_in": w_in[l].astype(BF16),
        "g_cq": g_cq[l].reshape(1, Q_LORA),
        "g_ckv": g_ckv[l].reshape(1, KV_LORA),
        "wqn": w_uq[l][:, :, :MLA_NOPE].reshape(Q_LORA, MLA_HEADS * MLA_NOPE).astype(BF16),
        "wqr": w_uq[l][:, :, MLA_NOPE:].reshape(Q_LORA, MLA_HEADS * MLA_ROPE).astype(BF16),
        "wk": w_uk[l].reshape(KV_LORA, MLA_HEADS * MLA_NOPE).astype(BF16),
        "wqt": jnp.transpose(w_uq[l].reshape(Q_LORA, MLA_HEADS * MLA_QK)).astype(BF16),
        "wvt": jnp.transpose(w_uv[l].reshape(KV_LORA, MLA_HEADS * MLA_DV)).astype(BF16),
        "wkt": jnp.transpose(w_uk[l], (1, 2, 0)).astype(BF16),
        "wvh": jnp.transpose(w_uv[l], (1, 0, 2)).astype(BF16),
        "g_ret": g_ret[l].reshape(1, RET_WIDTH),
        "b_ret": b_ret[l].reshape(1, RET_WIDTH),
        "ln1_g": ln1_g[l].reshape(1, 1, D_MODEL),
        "ln1_b": ln1_b[l].reshape(1, 1, D_MODEL),
        "late_f32": {"w_out": w_out[l], "w_gate": w_gate[l], "w_up": w_up[l], "w_down": w_down[l]},
        "ln2_g": ln2_g[l].reshape(1, 1, D_MODEL),
        "ln2_b": ln2_b[l].reshape(1, 1, D_MODEL),
    }


def kernel(x_prompt, x_sample, c_prompt, c_sample, cache_mla_ckv, cache_mla_krope, state_ret, w_ada, b_ada, w_in, g_cq, g_ckv, w_uq, w_uk, w_uv, g_ret, b_ret, w_out, ln1_g, ln1_b, w_gate, w_up, w_down, ln2_g, ln2_b):
    bp, tp, _ = x_prompt.shape
    bs, ts, _ = x_sample.shape
    past = cache_mla_ckv.shape[2]
    c_all = jnp.concatenate([c_prompt, c_sample], axis=0)
    yp, ys = x_prompt, x_sample
    outs = [[] for _ in range(6)]
    for l in range(w_ada.shape[0]):
        wts = _prep_weights(l, w_in, g_cq, g_ckv, w_uq, w_uk, w_uv, g_ret, b_ret, w_out, ln1_g, ln1_b,
                            w_gate, w_up, w_down, ln2_g, ln2_b)
        mod = _ada(c_all, w_ada[l], b_ada[l]).reshape(bp + bs, 6, 1, D_MODEL)
        mods_p = tuple(mod[:bp, i] for i in range(6))
        mods_s = tuple(mod[bp:, i] for i in range(6))
        s0 = jnp.zeros((bp, RET_HEADS, RET_DK, RET_DV), F32)
        yp, ckv_p, kr_p, rs_p = _layer(yp, mods_p, 0, None, None, s0, wts)
        ys, ckv_s, kr_s, rs_s = _layer(ys, mods_s, past, cache_mla_ckv[l], cache_mla_krope[l],
                                       state_ret[l], wts)
        for lst, val in zip(outs, (ckv_p, kr_p, rs_p, ckv_s, kr_s, rs_s)):
            lst.append(val)
    stacked = [v[0][None] if len(v) == 1 else jnp.stack(v, axis=0) for v in outs]
    return (yp, ys, *stacked)
```

```python
import functools
import math

import jax
import jax.numpy as jnp
import numpy as np
from jax import lax
from jax.experimental import pallas as pl
from jax.experimental.pallas import tpu as pltpu

F32 = jnp.float32
BF16 = jnp.bfloat16

D_MODEL = 2048
CHUNK = 64
RET_HEADS = 8
RET_DK = 64
RET_DV = 128
RET_QK = RET_HEADS * RET_DK
RET_WIDTH = RET_HEADS * RET_DV
MLA_HEADS = 8
MLA_NOPE = 128
MLA_ROPE = 64
MLA_DV = 128
MLA_DV_AUG = MLA_DV + 16
MLA_QK = MLA_NOPE + MLA_ROPE
MLA_WIDTH = MLA_HEADS * MLA_DV
Q_LORA = 512
KV_LORA = 512
MLA_SCALE = MLA_QK ** -0.5
D_FF = 5632
DEPTH = 1
ALPHA = (2 * DEPTH) ** 0.25
ROPE_BASE = 10000.0
EPS = 1e-5
LOG2E = math.log2(math.e)
NEG_BIG = -1e30

LANES = 128
BF16_SUBLANES = 16
ROW_TILE = 512
FFN_ROW_TILE = 1024
FFN_COL_TILE = 512
RET_CHUNK = 256
RET_CHUNKS_PER_STEP = 2
ATTN_Q_TILE = 1024
ATTN_K_TILE = 512
ATTN_HEADS_PER_STEP = 2
ADA_COL_TILE = 1024
VMEM_LIMIT = 56 * 1024 * 1024
FFN_VMEM_LIMIT = 60 * 1024 * 1024


def _params(sem, vmem=VMEM_LIMIT):
    return pltpu.CompilerParams(dimension_semantics=sem, vmem_limit_bytes=vmem)


def _const_spec(shape):
    zeros = (0,) * len(shape)
    return pl.BlockSpec(shape, lambda *_: zeros, pipeline_mode=pl.Buffered(1))


def _dot(a, b):
    return jnp.dot(a, b, preferred_element_type=F32)


def _dot_nt(a, b):
    return lax.dot_general(a, b, (((1,), (1,)), ((), ())), preferred_element_type=F32)


def _dot_tn(a, b):
    return lax.dot_general(a, b, (((0,), (0,)), ((), ())), preferred_element_type=F32)


def _silu(x):
    return x * jax.nn.sigmoid(x)


def _row_tiling(batch, seq, rows):
    if seq >= rows:
        assert seq % rows == 0
        return 1, rows
    nb = min(batch, rows // seq)
    assert batch % nb == 0
    return nb, seq


def _half_tiles(nb, rb):
    if nb > 1:
        return [(slice(h * nb // 2, (h + 1) * nb // 2), slice(None), nb // 2, rb) for h in range(2)]
    return [(slice(None), slice(h * rb // 2, (h + 1) * rb // 2), nb, rb // 2) for h in range(2)]


def _rope_cols(y, cos, sin, first_half):
    outs = []
    for g in range(y.shape[1] // LANES):
        yg = y[:, g * LANES:(g + 1) * LANES]
        rot = jnp.where(first_half, pltpu.roll(yg, LANES - 32, 1), pltpu.roll(yg, 32, 1))
        outs.append(yg * cos + rot * sin)
    return outs[0] if len(outs) == 1 else jnp.concatenate(outs, axis=1)


def _rope_angles(pos0, n):
    half = 32
    inv = ROPE_BASE ** (-np.arange(half, dtype=np.float64) / half)
    pos = (pos0 + np.arange(n)).astype(np.float64)
    return pos[:, None] * inv[None, :]


def _rope_tables(pos0, n):
    ang = _rope_angles(pos0, n)
    cos = np.cos(ang).astype(np.float32)
    sin = np.sin(ang).astype(np.float32)
    return np.concatenate([cos, cos, cos, cos], axis=1), np.concatenate([-sin, sin, -sin, sin], axis=1)


def _rope_tables_t(pos0, n):
    ang = _rope_angles(pos0, n)
    return (np.ascontiguousarray(np.cos(ang).T.astype(np.float32)),
            np.ascontiguousarray(np.sin(ang).T.astype(np.float32)))


def _tile_rows(t, nb):
    if nb == 1:
        return t
    return jnp.broadcast_to(t[None], (nb,) + t.shape).reshape(nb * t.shape[0], t.shape[1])


def _ada_kernel(c_ref, w_ref, b_ref, o_ref):
    c = c_ref[...]
    a = _silu(c).astype(BF16)
    o_ref[...] = _dot(a, w_ref[...].astype(BF16)) + b_ref[...]


def _ada(c, w_ada, b_ada):
    nrow = c.shape[0]
    ncol = w_ada.shape[1]
    tn = ADA_COL_TILE
    return pl.pallas_call(
        _ada_kernel,
        grid=(ncol // tn,),
        in_specs=[_const_spec((nrow, D_MODEL)),
                  pl.BlockSpec((D_MODEL, tn), lambda j: (0, j)),
                  pl.BlockSpec((1, tn), lambda j: (0, j))],
        out_specs=pl.BlockSpec((nrow, tn), lambda j: (0, j)),
        out_shape=jax.ShapeDtypeStruct((nrow, ncol), F32),
        compiler_params=_params(("arbitrary",)),
        name="ada_mod",
    )(c, w_ada, b_ada.reshape(1, ncol))


IN_W = 2 * RET_QK + 2 * RET_WIDTH + Q_LORA + KV_LORA + MLA_ROPE


def _inproj_kernel(x_ref, sc_ref, sh_ref, w_ref, gcq_ref, gckv_ref, cos_ref, sin_ref, *rest, nb, rb, ncast):
    cast_in = rest[:ncast]
    rq_ref, rk_ref, rv_ref, rg_ref, cq_ref, ckv_ref, kr_ref = rest[ncast:ncast + 7]
    cast_out = rest[ncast + 7:]
    for src, dst in zip(cast_in, cast_out):
        dst[...] = src[0].astype(BF16)

    def rmsnorm(y, g):
        ms = jnp.mean(y * y, axis=-1, keepdims=True)
        return y * lax.rsqrt(ms + EPS) * g

    for bsl, rsl, hb_, hr in _half_tiles(nb, rb):
        m = hb_ * hr
        h = x_ref[bsl, rsl, :] * (1.0 + sc_ref[bsl]) + sh_ref[bsl]
        hb = h.astype(BF16).reshape(m, D_MODEL)
        cos = _tile_rows(cos_ref[rsl, :], hb_)
        sin = _tile_rows(sin_ref[rsl, :], hb_)
        lane = lax.broadcasted_iota(jnp.int32, (m, LANES), 1)
        first_half = (lane % 64) < 32

        def proj(lo, hi):
            return _dot(hb, w_ref[:, lo:hi])

        def put(ref, y):
            ref[bsl, rsl, :] = y.reshape(hb_, hr, y.shape[-1])

        o = 0
        put(rq_ref, (_rope_cols(proj(o, o + RET_QK), cos, sin, first_half) * (RET_DK ** -0.5)).astype(BF16))
        o += RET_QK
        put(rk_ref, _rope_cols(proj(o, o + RET_QK), cos, sin, first_half).astype(BF16))
        o += RET_QK
        put(rv_ref, proj(o, o + RET_WIDTH).astype(BF16))
        o += RET_WIDTH
        put(rg_ref, proj(o, o + RET_WIDTH).astype(BF16))
        o += RET_WIDTH
        put(cq_ref, rmsnorm(proj(o, o + Q_LORA), gcq_ref[...]).astype(BF16))
        o += Q_LORA
        put(ckv_ref, rmsnorm(proj(o, o + KV_LORA), gckv_ref[...]))
        o += KV_LORA
        kr = proj(o, o + MLA_ROPE)
        kr = _rope_cols(jnp.concatenate([kr, kr], axis=1), cos, sin, first_half)
        put(kr_ref, kr[:, :MLA_ROPE])


def _inproj(x, sc, sh, w_in_b, g_cq, g_ckv, cos, sin, cast=(), layer=0):
    batch, seq, _ = x.shape
    nb, rb = _row_tiling(batch, seq, ROW_TILE)
    grid = (batch // nb, seq // rb)
    nsteps = grid[0] * grid[1]

    def cast_specs(w):
        rows, share = w.shape[1] // nsteps, 1
        while rows % BF16_SUBLANES or w.shape[1] % rows:
            rows, share = rows * 2, share * 2
        assert nsteps % share == 0
        step = lambda b, i: (b * grid[1] + i) // share
        return (pl.BlockSpec((1, rows, w.shape[2]), lambda b, i: (layer, step(b, i), 0)),
                pl.BlockSpec((rows, w.shape[2]), lambda b, i: (step(b, i), 0)))

    cast_io = [cast_specs(w) for w in cast]
    row = lambda w: pl.BlockSpec((nb, rb, w), lambda b, i: (b, i, 0))
    mod = pl.BlockSpec((nb, 1, D_MODEL), lambda b, i: (b, 0, 0))
    tab = pl.BlockSpec((rb, LANES), lambda b, i: (i, 0))
    sds = lambda w, dt: jax.ShapeDtypeStruct((batch, seq, w), dt)
    return pl.pallas_call(
        functools.partial(_inproj_kernel, nb=nb, rb=rb, ncast=len(cast)),
        grid=grid,
        in_specs=[row(D_MODEL), mod, mod, _const_spec((D_MODEL, IN_W)),
                  _const_spec((1, Q_LORA)), _const_spec((1, KV_LORA)), tab, tab] + [io[0] for io in cast_io],
        out_specs=[row(RET_QK), row(RET_QK), row(RET_WIDTH), row(RET_WIDTH),
                   row(Q_LORA), row(KV_LORA), row(MLA_ROPE)] + [io[1] for io in cast_io],
        out_shape=[sds(RET_QK, BF16), sds(RET_QK, BF16), sds(RET_WIDTH, BF16), sds(RET_WIDTH, BF16),
                   sds(Q_LORA, BF16), sds(KV_LORA, F32), sds(MLA_ROPE, F32)]
                  + [jax.ShapeDtypeStruct(w.shape[1:], BF16) for w in cast],
        compiler_params=_params(("arbitrary", "arbitrary") if cast else ("parallel", "parallel")),
        name="in_proj",
    )(x, sc, sh, w_in_b, g_cq, g_ckv, cos, sin, *cast)


def _ret_tables(c):
    hh = np.arange(RET_HEADS, dtype=np.float64)
    logg = np.log1p(-np.exp2(-5.0 - hh))
    i = np.arange(c, dtype=np.float64)
    diff = i[:, None] - i[None, :]
    dmask = np.where(diff[None] >= 0.0, np.exp(np.maximum(diff, 0.0)[None] * logg[:, None, None]), 0.0)
    qdec = np.exp((i + 1.0)[:, None] * logg[None, :])
    kdec = np.exp((c - 1.0 - i)[:, None] * logg[None, :])
    cdec = np.exp(c * logg)
    qdec = np.repeat(qdec, RET_DK, axis=1)
    kdec = np.repeat(kdec, RET_DK, axis=1)
    r = np.arange(2 * RET_DK)[:, None] // RET_DK
    col = np.arange(2 * RET_DV)[None, :] // RET_DV
    pair = np.arange(RET_HEADS // 2)[:, None, None]
    sdec = np.where(r[None] == col[None], cdec[2 * pair + r[None]], 0.0)
    return tuple(a.astype(np.float32) for a in (dmask, qdec, kdec, sdec))


def _ret_kernel(rq_ref, rk_ref, rv_ref, rg_ref, s0_ref, dmask_ref, qdec_ref, kdec_ref, sdec_ref,
                gret_ref, bret_ref, o_ref, so_ref, s_scr, *, c, nc):
    n = pl.program_id(1)
    npair = RET_HEADS // 2

    @pl.when(n == 0)
    def _():
        s_scr[...] = jnp.zeros(s_scr.shape, F32)
        for p in range(npair):
            s_scr[p, 0:RET_DK, 0:RET_DV] = s0_ref[0, 2 * p]
            s_scr[p, RET_DK:2 * RET_DK, RET_DV:2 * RET_DV] = s0_ref[0, 2 * p + 1]

    for ci in range(nc):
        rows = slice(ci * c, (ci + 1) * c)
        q = rq_ref[0, rows]
        k = rk_ref[0, rows]
        qd = (q.astype(F32) * qdec_ref[...]).astype(BF16)
        kd = (k.astype(F32) * kdec_ref[...]).astype(BF16)
        lane = lax.broadcasted_iota(jnp.int32, (c, LANES), 1)
        lo = lane < RET_DK
        zero = jnp.zeros((c, LANES), BF16)
        for p in range(npair):
            qp = q[:, p * LANES:(p + 1) * LANES]
            kp = k[:, p * LANES:(p + 1) * LANES]
            vp = rv_ref[0, rows, p * 2 * RET_DV:(p + 1) * 2 * RET_DV]
            sc0 = _dot_nt(jnp.where(lo, qp, zero), kp) * dmask_ref[2 * p]
            sc1 = _dot_nt(jnp.where(lo, zero, qp), kp) * dmask_ref[2 * p + 1]
            intra0 = _dot(sc0.astype(BF16), vp[:, :RET_DV])
            intra1 = _dot(sc1.astype(BF16), vp[:, RET_DV:])
            s = s_scr[p]
            inter = _dot(qd[:, p * LANES:(p + 1) * LANES], s.astype(BF16))
            kv = _dot_tn(kd[:, p * LANES:(p + 1) * LANES], vp)
            sdec = sdec_ref[p]
            s_scr[p] = s * sdec + jnp.where(sdec > 0.0, kv, 0.0)
            for hh in range(2):
                hd = 2 * p + hh
                cols = slice(hd * RET_DV, (hd + 1) * RET_DV)
                oh = (intra0 if hh == 0 else intra1) + inter[:, hh * RET_DV:(hh + 1) * RET_DV]
                mu = jnp.mean(oh, axis=-1, keepdims=True)
                d = oh - mu
                var = jnp.mean(d * d, axis=-1, keepdims=True)
                y = d * lax.rsqrt(var + EPS) * gret_ref[:, cols] + bret_ref[:, cols]
                gate = rg_ref[0, rows, cols].astype(F32)
                o_ref[0, rows, cols] = (_silu(gate) * y).astype(BF16)

    @pl.when(n == pl.num_programs(1) - 1)
    def _():
        for p in range(npair):
            so_ref[0, 2 * p] = s_scr[p, 0:RET_DK, 0:RET_DV]
            so_ref[0, 2 * p + 1] = s_scr[p, RET_DK:2 * RET_DK, RET_DV:2 * RET_DV]


def _retention(rq, rk, rv, rg, s0, g_ret, b_ret):
    batch, seq, _ = rq.shape
    c = min(RET_CHUNK, seq)
    nc = min(RET_CHUNKS_PER_STEP, seq // c)
    assert seq % (c * nc) == 0
    dmask, qdec, kdec, sdec = _ret_tables(c)
    row = lambda w: pl.BlockSpec((1, c * nc, w), lambda b, n: (b, n, 0))
    st = pl.BlockSpec((1, RET_HEADS, RET_DK, RET_DV), lambda b, n: (b, 0, 0, 0))
    return pl.pallas_call(
        functools.partial(_ret_kernel, c=c, nc=nc),
        grid=(batch, seq // (c * nc)),
        in_specs=[row(RET_QK), row(RET_QK), row(RET_WIDTH), row(RET_WIDTH), st,
                  _const_spec(dmask.shape), _const_spec(qdec.shape), _const_spec(kdec.shape),
                  _const_spec(sdec.shape), _const_spec((1, RET_WIDTH)), _const_spec((1, RET_WIDTH))],
        out_specs=[row(RET_WIDTH), st],
        out_shape=[jax.ShapeDtypeStruct((batch, seq, RET_WIDTH), BF16),
                   jax.ShapeDtypeStruct((batch, RET_HEADS, RET_DK, RET_DV), F32)],
        scratch_shapes=[pltpu.VMEM((RET_HEADS // 2, 2 * RET_DK, 2 * RET_DV), F32)],
        compiler_params=_params(("parallel", "arbitrary")),
        name="retention",
    )(rq, rk, rv, rg, s0, dmask, qdec, kdec, sdec, g_ret, b_ret)


def _upproj_kernel(cq_ref, ckv_ref, kr_ref, cos_ref, sin_ref, wqt_ref, wk_ref, wvt_ref,
                   qt_ref, k_ref, vt_ref):
    cq = cq_ref[0]
    ckv = ckv_ref[0].astype(BF16)
    krb = kr_ref[0].astype(BF16)
    cos = cos_ref[...]
    sin = sin_ref[...]
    half = MLA_ROPE // 2
    qt = _dot_nt(wqt_ref[...], cq) * (MLA_SCALE * LOG2E)
    kn = _dot(ckv, wk_ref[...]).astype(BF16)
    vt = _dot_nt(wvt_ref[...], ckv).astype(BF16)
    for hd in range(MLA_HEADS):
        r0 = hd * MLA_QK
        x1 = qt[r0 + MLA_NOPE:r0 + MLA_NOPE + half]
        x2 = qt[r0 + MLA_NOPE + half:r0 + MLA_QK]
        qt_ref[0, hd, 0:MLA_NOPE] = qt[r0:r0 + MLA_NOPE].astype(BF16)
        qt_ref[0, hd, MLA_NOPE:MLA_NOPE + half] = (x1 * cos - x2 * sin).astype(BF16)
        qt_ref[0, hd, MLA_NOPE + half:MLA_QK] = (x2 * cos + x1 * sin).astype(BF16)
        k_ref[0, hd, :, 0:MLA_NOPE] = kn[:, hd * MLA_NOPE:(hd + 1) * MLA_NOPE]
        k_ref[0, hd, :, MLA_NOPE:MLA_QK] = krb
        vt_ref[0, hd, 0:MLA_DV] = vt[hd * MLA_DV:(hd + 1) * MLA_DV]
        vt_ref[0, hd, MLA_DV:MLA_DV_AUG] = jnp.ones((MLA_DV_AUG - MLA_DV, vt.shape[1]), BF16)


def _upproj(cq, ckv, kr, cos_t, sin_t, wqt, wk, wvt):
    batch, seq, _ = cq.shape
    rb = min(ROW_TILE, seq)
    assert seq % rb == 0
    row = lambda w: pl.BlockSpec((1, rb, w), lambda b, i: (b, i, 0))
    tab = pl.BlockSpec((MLA_ROPE // 2, rb), lambda b, i: (0, i))
    head_t = lambda w: pl.BlockSpec((1, MLA_HEADS, w, rb), lambda b, i: (b, 0, 0, i))
    return pl.pallas_call(
        _upproj_kernel,
        grid=(batch, seq // rb),
        in_specs=[row(Q_LORA), row(KV_LORA), row(MLA_ROPE), tab, tab,
                  _const_spec(wqt.shape), _const_spec(wk.shape), _const_spec(wvt.shape)],
        out_specs=[head_t(MLA_QK),
                   pl.BlockSpec((1, MLA_HEADS, rb, MLA_QK), lambda b, i: (b, 0, i, 0)),
                   head_t(MLA_DV_AUG)],
        out_shape=[jax.ShapeDtypeStruct((batch, MLA_HEADS, MLA_QK, seq), BF16),
                   jax.ShapeDtypeStruct((batch, MLA_HEADS, seq, MLA_QK), BF16),
                   jax.ShapeDtypeStruct((batch, MLA_HEADS, MLA_DV_AUG, seq), BF16)],
        compiler_params=_params(("parallel", "parallel")),
        name="mla_up_proj",
    )(cq, ckv, kr, cos_t, sin_t, wqt, wk, wvt)


def _flash_kernel(qt_ref, k_ref, vt_ref, o_ref, m_scr, acc_scr, sa_scr, sb_scr, ma_scr, mb_scr, *, tq, tk):
    i = pl.program_id(2)
    ratio = tq // tk
    m_scr[...] = jnp.full(m_scr.shape, -jnp.inf, F32)
    acc_scr[...] = jnp.zeros(acc_scr.shape, F32)

    heads = range(ATTN_HEADS_PER_STEP)

    def scores(j, hd, c0=0):
        start = pl.multiple_of(j * tk, tk)
        return _dot(k_ref[0, hd, pl.ds(start, tk), :], qt_ref[0, hd, :, c0:])

    def update(buf, j, hd, diag=None):
        c0 = 0 if diag is None else diag * tk
        start = pl.multiple_of(j * tk, tk)
        vtb = vt_ref[0, hd, :, pl.ds(start, tk)]
        s = buf[0][hd, :, c0:]
        if diag is None:
            smax = buf[1][hd]
        else:
            kc = (c0 + lax.broadcasted_iota(jnp.int32, (tk, tq - c0), 0)) // CHUNK
            qc = (c0 + lax.broadcasted_iota(jnp.int32, (tk, tq - c0), 1)) // CHUNK
            s = jnp.where(kc <= qc, s, NEG_BIG)
            smax = jnp.max(s, axis=0, keepdims=True)
        m_prev = m_scr[hd, :, c0:]
        m_new = jnp.maximum(m_prev, smax)
        alpha = jnp.exp2(m_prev - m_new)
        p = jnp.exp2(s - m_new)
        acc_scr[hd, :, c0:] = alpha * acc_scr[hd, :, c0:] + _dot(vtb, p.astype(BF16))
        m_scr[hd, :, c0:] = m_new

    def produce(buf, j, hd, diag=None):
        c0 = 0 if diag is None else diag * tk
        s = scores(j, hd, c0)
        buf[0][hd, :, c0:] = s
        if diag is None:
            buf[1][hd] = jnp.max(s, axis=0, keepdims=True)

    def stage(cur, nxt, j, diag=None, diag_next=None):
        for hd in heads:
            produce(nxt, j + 1, hd, diag_next)
            update(cur, j, hd, diag)

    buf_a = (sa_scr, ma_scr)
    buf_b = (sb_scr, mb_scr)
    for hd in heads:
        produce(buf_a, 0, hd)

    def body(jj, carry):
        stage(buf_a, buf_b, 2 * jj)
        stage(buf_b, buf_a, 2 * jj + 1)
        return carry

    nfull = i * ratio
    lax.fori_loop(0, i * (ratio // 2), body, 0)
    bufs = (buf_a, buf_b)
    for d in range(ratio - 1):
        stage(bufs[d % 2], bufs[(d + 1) % 2], nfull + d, d, d + 1)
    for hd in heads:
        update(bufs[(ratio - 1) % 2], nfull + ratio - 1, hd, ratio - 1)

    for hd in heads:
        out_t = acc_scr[hd, 0:MLA_DV] / acc_scr[hd, MLA_DV:MLA_DV + 1]
        o_ref[0, :, hd * MLA_DV:(hd + 1) * MLA_DV] = jnp.transpose(out_t).astype(BF16)


def _flash(qt, k, vt):
    batch, heads, seq, _ = k.shape
    tq = min(ATTN_Q_TILE, seq)
    tk = min(ATTN_K_TILE, seq // 2)
    g = ATTN_HEADS_PER_STEP
    assert seq % tq == 0 and tq % (2 * tk) == 0 and tk % CHUNK == 0 and heads % g == 0
    return pl.pallas_call(
        functools.partial(_flash_kernel, tq=tq, tk=tk),
        grid=(batch, heads // g, seq // tq),
        in_specs=[pl.BlockSpec((1, g, MLA_QK, tq), lambda b, h, i: (b, h, 0, i)),
                  pl.BlockSpec((1, g, seq, MLA_QK), lambda b, h, i: (b, h, 0, 0)),
                  pl.BlockSpec((1, g, MLA_DV_AUG, seq), lambda b, h, i: (b, h, 0, 0))],
        out_specs=pl.BlockSpec((1, tq, g * MLA_DV), lambda b, h, i: (b, i, h)),
        out_shape=jax.ShapeDtypeStruct((batch, seq, MLA_WIDTH), BF16),
        scratch_shapes=[pltpu.VMEM((g, 1, tq), F32),
                        pltpu.VMEM((g, MLA_DV_AUG, tq), F32), pltpu.VMEM((g, tk, tq), F32),
                        pltpu.VMEM((g, tk, tq), F32), pltpu.VMEM((g, 1, tq), F32), pltpu.VMEM((g, 1, tq), F32)],
        compiler_params=_params(("parallel", "parallel", "arbitrary")),
        name="mla_flash",
    )(qt, k, vt)


def _cached_attn_kernel(cq_ref, ckvn_ref, krn_ref, ckvp_ref, krp_ref, cos_ref, sin_ref,
                        wqn_ref, wqr_ref, wkt_ref, wv_ref, o_ref, *, tq, past):
    cq = cq_ref[0]
    lane = lax.broadcasted_iota(jnp.int32, (tq, LANES), 1)
    first_half = (lane % 64) < 32
    qscale = MLA_SCALE * LOG2E
    qn = (_dot(cq, wqn_ref[...]) * qscale).astype(BF16)
    qr = (_rope_cols(_dot(cq, wqr_ref[...]), cos_ref[...], sin_ref[...], first_half) * qscale).astype(BF16)
    qlat = jnp.concatenate(
        [_dot(qn[:, hd * MLA_NOPE:(hd + 1) * MLA_NOPE], wkt_ref[hd]) for hd in range(MLA_HEADS)],
        axis=0).astype(BF16)
    qrr = jnp.concatenate([qr[:, hd * MLA_ROPE:(hd + 1) * MLA_ROPE] for hd in range(MLA_HEADS)], axis=0)
    ckv_p = ckvp_ref[0].astype(BF16)
    kr_p = krp_ref[0].astype(BF16)
    ckv_n = ckvn_ref[0].astype(BF16)
    kr_n = krn_ref[0].astype(BF16)
    s_p = _dot_nt(qlat, ckv_p) + _dot_nt(qrr, kr_p)
    s_n = _dot_nt(qlat, ckv_n) + _dot_nt(qrr, kr_n)
    rows = MLA_HEADS * tq
    qpos = past + lax.broadcasted_iota(jnp.int32, (rows, tq), 0) % tq
    kpos = past + lax.broadcasted_iota(jnp.int32, (rows, tq), 1)
    s_n = jnp.where(kpos // CHUNK <= qpos // CHUNK, s_n, NEG_BIG)
    m = jnp.maximum(jnp.max(s_p, axis=1, keepdims=True), jnp.max(s_n, axis=1, keepdims=True))
    p_p = jnp.exp2(s_p - m)
    p_n = jnp.exp2(s_n - m)
    l = jnp.sum(p_p, axis=1, keepdims=True) + jnp.sum(p_n, axis=1, keepdims=True)
    olat = (_dot(p_p.astype(BF16), ckv_p) + _dot(p_n.astype(BF16), ckv_n)) / l
    olat = olat.astype(BF16)
    for hd in range(MLA_HEADS):
        o_ref[0, :, hd * MLA_DV:(hd + 1) * MLA_DV] = _dot(olat[hd * tq:(hd + 1) * tq], wv_ref[hd]).astype(BF16)


def _cached_attn(cq, ckv_new, kr_new, ckv_past, kr_past, cos, sin, wqn, wqr, wkt, wvh):
    batch, tq, _ = cq.shape
    past = ckv_past.shape[1]
    assert past % CHUNK == 0
    one = lambda n, w: pl.BlockSpec((1, n, w), lambda b: (b, 0, 0))
    return pl.pallas_call(
        functools.partial(_cached_attn_kernel, tq=tq, past=past),
        grid=(batch,),
        in_specs=[one(tq, Q_LORA), one(tq, KV_LORA), one(tq, MLA_ROPE), one(past, KV_LORA), one(past, MLA_ROPE),
                  _const_spec(cos.shape), _const_spec(sin.shape), _const_spec(wqn.shape), _const_spec(wqr.shape),
                  _const_spec(wkt.shape), _const_spec(wvh.shape)],
        out_specs=one(tq, MLA_WIDTH),
        out_shape=jax.ShapeDtypeStruct((batch, tq, MLA_WIDTH), BF16),
        compiler_params=_params(("parallel",)),
        name="mla_cached",
    )(cq, ckv_new, kr_new, ckv_past, kr_past, cos, sin, wqn, wqr, wkt, wvh)


def _layernorm(y, g, b):
    mu = jnp.mean(y, axis=-1, keepdims=True)
    d = y - mu
    var = jnp.mean(d * d, axis=-1, keepdims=True)
    return d * lax.rsqrt(var + EPS) * g + b


def _outproj_kernel(oret_ref, omla_ref, x_ref, g1_ref, w_ref, lng_ref, lnb_ref, x1_ref, *, nb, rb):
    for bsl, rsl, hb, hr in _half_tiles(nb, rb):
        a = oret_ref[bsl, rsl, :].reshape(hb * hr, RET_WIDTH)
        b = omla_ref[bsl, rsl, :].reshape(hb * hr, MLA_WIDTH)
        mix = _dot(a, w_ref[0:RET_WIDTH, :]) + _dot(b, w_ref[RET_WIDTH:RET_WIDTH + MLA_WIDTH, :])
        y = ALPHA * x_ref[bsl, rsl, :] + g1_ref[bsl] * mix.reshape(hb, hr, D_MODEL)
        x1_ref[bsl, rsl, :] = _layernorm(y, lng_ref[...], lnb_ref[...])


def _outproj(o_ret, o_mla, x, g1, w_out_b, ln_g, ln_b):
    batch, seq, _ = x.shape
    nb, rb = _row_tiling(batch, seq, ROW_TILE)
    row = lambda w: pl.BlockSpec((nb, rb, w), lambda b, i: (b, i, 0))
    mod = pl.BlockSpec((nb, 1, D_MODEL), lambda b, i: (b, 0, 0))
    return pl.pallas_call(
        functools.partial(_outproj_kernel, nb=nb, rb=rb),
        grid=(batch // nb, seq // rb),
        in_specs=[row(RET_WIDTH), row(MLA_WIDTH), row(D_MODEL), mod, _const_spec(w_out_b.shape),
                  _const_spec((1, 1, D_MODEL)), _const_spec((1, 1, D_MODEL))],
        out_specs=row(D_MODEL),
        out_shape=jax.ShapeDtypeStruct((batch, seq, D_MODEL), F32),
        compiler_params=_params(("parallel", "parallel")),
        name="out_proj_ln1",
    )(o_ret, o_mla, x, g1, w_out_b, ln_g, ln_b)


def _ffn_kernel(x1_ref, sc_ref, sh_ref, g2_ref, wg_ref, wu_ref, wd_ref, lng_ref, lnb_ref, o_ref, h_scr,
                *, nb, rb):
    m = nb * rb
    j = pl.program_id(2)

    @pl.when(j == 0)
    def _():
        x1 = x1_ref[...]
        h_scr[...] = (x1 * (1.0 + sc_ref[...]) + sh_ref[...]).astype(BF16).reshape(m, D_MODEL)
        o_ref[...] = ALPHA * x1

    h = h_scr[...]
    act = (_silu(_dot(h, wg_ref[...])) * _dot(h, wu_ref[...])).astype(BF16)
    for n in range(D_MODEL // FFN_COL_TILE):
        cols = slice(n * FFN_COL_TILE, (n + 1) * FFN_COL_TILE)
        part = _dot(act, wd_ref[:, cols]).reshape(nb, rb, FFN_COL_TILE)
        o_ref[:, :, cols] += part * g2_ref[:, :, cols]

    @pl.when(j == pl.num_programs(2) - 1)
    def _():
        o_ref[...] = _layernorm(o_ref[...], lng_ref[...], lnb_ref[...])


def _ffn(x1, sc, sh, g2, wg, wu, wd, ln_g, ln_b):
    batch, seq, _ = x1.shape
    nb, rb = _row_tiling(batch, seq, FFN_ROW_TILE)
    tf = FFN_COL_TILE
    assert D_FF % tf == 0
    row = pl.BlockSpec((nb, rb, D_MODEL), lambda b, i, j: (b, i, 0))
    mod = pl.BlockSpec((nb, 1, D_MODEL), lambda b, i, j: (b, 0, 0))
    return pl.pallas_call(
        functools.partial(_ffn_kernel, nb=nb, rb=rb),
        grid=(batch // nb, seq // rb, D_FF // tf),
        in_specs=[row, mod, mod, mod,
                  pl.BlockSpec((D_MODEL, tf), lambda b, i, j: (0, j)),
                  pl.BlockSpec((D_MODEL, tf), lambda b, i, j: (0, j)),
                  pl.BlockSpec((tf, D_MODEL), lambda b, i, j: (j, 0)),
                  _const_spec((1, 1, D_MODEL)), _const_spec((1, 1, D_MODEL))],
        out_specs=row,
        out_shape=jax.ShapeDtypeStruct((batch, seq, D_MODEL), F32),
        scratch_shapes=[pltpu.VMEM((nb * rb, D_MODEL), BF16)],
        compiler_params=_params(("parallel", "parallel", "arbitrary"), FFN_VMEM_LIMIT),
        name="swiglu_ln2",
    )(x1, sc, sh, g2, wg, wu, wd, ln_g, ln_b)


def _layer(x, mods, pos0, ckv_past, kr_past, ret_s0, wts):
    sh1, sc1, g1, sh2, sc2, g2 = mods
    seq = x.shape[1]
    cos, sin = _rope_tables(pos0, seq)
    layer, late = wts.pop("late_f32", (0, {}))
    outs = _inproj(x, sc1, sh1, wts["w_in"], wts["g_cq"], wts["g_ckv"], cos, sin, tuple(late.values()), layer)
    rq, rk, rv, rg, cq, ckv, kr = outs[:7]
    wts.update(zip(late.keys(), outs[7:]))
    o_ret, ret_s1 = _retention(rq, rk, rv, rg, ret_s0, wts["g_ret"], wts["b_ret"])
    if ckv_past is None:
        cos_t, sin_t = _rope_tables_t(pos0, seq)
        qt, k, vt = _upproj(cq, ckv, kr, cos_t, sin_t, wts["wqt"], wts["wk"], wts["wvt"])
        o_mla = _flash(qt, k, vt)
    else:
        o_mla = _cached_attn(cq, ckv, kr, ckv_past, kr_past, cos, sin,
                             wts["wqn"], wts["wqr"], wts["wkt"], wts["wvh"])
    x1 = _outproj(o_ret, o_mla, x, g1, wts["w_out"], wts["ln1_g"], wts["ln1_b"])
    y = _ffn(x1, sc2, sh2, g2, wts["w_gate"], wts["w_up"], wts["w_down"], wts["ln2_g"], wts["ln2_b"])
    return y, ckv, kr, ret_s1


def _prep_weights(l, w_in, g_cq, g_ckv, w_uq, w_uk, w_uv, g_ret, b_ret, w_out, ln1_g, ln1_b,
                  w_gate, w_up, w_down, ln2_g, ln2_b):
    return {
        "w_in": w_in[l].astype(BF16),
        "g_cq": g_cq[l].reshape(1, Q_LORA),
        "g_ckv": g_ckv[l].reshape(1, KV_LORA),
        "wqn": w_uq[l][:, :, :MLA_NOPE].reshape(Q_LORA, MLA_HEADS * MLA_NOPE).astype(BF16),
        "wqr": w_uq[l][:, :, MLA_NOPE:].reshape(Q_LORA, MLA_HEADS * MLA_ROPE).astype(BF16),
        "wk": w_uk[l].reshape(KV_LORA, MLA_HEADS * MLA_NOPE).astype(BF16),
        "wqt": jnp.transpose(w_uq[l].reshape(Q_LORA, MLA_HEADS * MLA_QK)).astype(BF16),
        "wvt": jnp.transpose(w_uv[l].reshape(KV_LORA, MLA_HEADS * MLA_DV)).astype(BF16),
        "wkt": jnp.transpose(w_uk[l], (1, 2, 0)).astype(BF16),
        "wvh": jnp.transpose(w_uv[l], (1, 0, 2)).astype(BF16),
        "g_ret": g_ret[l].reshape(1, RET_WIDTH),
        "b_ret": b_ret[l].reshape(1, RET_WIDTH),
        "ln1_g": ln1_g[l].reshape(1, 1, D_MODEL),
        "ln1_b": ln1_b[l].reshape(1, 1, D_MODEL),
        "late_f32": (l, {"w_out": w_out, "w_gate": w_gate, "w_up": w_up, "w_down": w_down}),
        "ln2_g": ln2_g[l].reshape(1, 1, D_MODEL),
        "ln2_b": ln2_b[l].reshape(1, 1, D_MODEL),
    }


def kernel(x_prompt, x_sample, c_prompt, c_sample, cache_mla_ckv, cache_mla_krope, state_ret, w_ada, b_ada, w_in, g_cq, g_ckv, w_uq, w_uk, w_uv, g_ret, b_ret, w_out, ln1_g, ln1_b, w_gate, w_up, w_down, ln2_g, ln2_b):
    bp, tp, _ = x_prompt.shape
    bs, ts, _ = x_sample.shape
    past = cache_mla_ckv.shape[2]
    c_all = jnp.concatenate([c_prompt, c_sample], axis=0)
    yp, ys = x_prompt, x_sample
    outs = [[] for _ in range(6)]
    for l in range(w_ada.shape[0]):
        wts = _prep_weights(l, w_in, g_cq, g_ckv, w_uq, w_uk, w_uv, g_ret, b_ret, w_out, ln1_g, ln1_b,
                            w_gate, w_up, w_down, ln2_g, ln2_b)
        mod = _ada(c_all, w_ada[l], b_ada[l]).reshape(bp + bs, 6, 1, D_MODEL)
        mods_p = tuple(mod[:bp, i] for i in range(6))
        mods_s = tuple(mod[bp:, i] for i in range(6))
        s0 = jnp.zeros((bp, RET_HEADS, RET_DK, RET_DV), F32)
        yp, ckv_p, kr_p, rs_p = _layer(yp, mods_p, 0, None, None, s0, wts)
        ys, ckv_s, kr_s, rs_s = _layer(ys, mods_s, past, cache_mla_ckv[l], cache_mla_krope[l],
                                       state_ret[l], wts)
        for lst, val in zip(outs, (ckv_p, kr_p, rs_p, ckv_s, kr_s, rs_s)):
            lst.append(val)
    stacked = [v[0][None] if len(v) == 1 else jnp.stack(v, axis=0) for v in outs]
    return (yp, ys, *stacked)
```

```python
import functools
import math

import jax
import jax.numpy as jnp
import numpy as np
from jax import lax
from jax.experimental import pallas as pl
from jax.experimental.pallas import tpu as pltpu

F32 = jnp.float32
BF16 = jnp.bfloat16

D_MODEL = 2048
CHUNK = 64
RET_HEADS = 8
RET_DK = 64
RET_DV = 128
RET_QK = RET_HEADS * RET_DK
RET_WIDTH = RET_HEADS * RET_DV
MLA_HEADS = 8
MLA_NOPE = 128
MLA_ROPE = 64
MLA_DV = 128
MLA_DV_AUG = MLA_DV + 16
MLA_QK = MLA_NOPE + MLA_ROPE
MLA_WIDTH = MLA_HEADS * MLA_DV
Q_LORA = 512
KV_LORA = 512
MLA_SCALE = MLA_QK ** -0.5
D_FF = 5632
DEPTH = 1
ALPHA = (2 * DEPTH) ** 0.25
ROPE_BASE = 10000.0
EPS = 1e-5
LOG2E = math.log2(math.e)
NEG_BIG = -1e30

LANES = 128
BF16_SUBLANES = 16
ROW_TILE = 512
FFN_ROW_TILE = 1024
FFN_COL_TILE = 512
RET_CHUNK = 256
RET_CHUNKS_PER_STEP = 4
UP_ROW_TILE = 1024
ATTN_Q_TILE = 1024
ATTN_K_TILE = 512
ATTN_HEADS_PER_STEP = 2
ADA_COL_TILE = 1024
VMEM_LIMIT = 56 * 1024 * 1024
FFN_VMEM_LIMIT = 60 * 1024 * 1024


def _params(sem, vmem=VMEM_LIMIT):
    return pltpu.CompilerParams(dimension_semantics=sem, vmem_limit_bytes=vmem)


def _const_spec(shape):
    zeros = (0,) * len(shape)
    return pl.BlockSpec(shape, lambda *_: zeros, pipeline_mode=pl.Buffered(1))


def _dot(a, b):
    return jnp.dot(a, b, preferred_element_type=F32)


def _dot_nt(a, b):
    return lax.dot_general(a, b, (((1,), (1,)), ((), ())), preferred_element_type=F32)


def _dot_tn(a, b):
    return lax.dot_general(a, b, (((0,), (0,)), ((), ())), preferred_element_type=F32)


def _silu(x):
    return x * jax.nn.sigmoid(x)


def _row_tiling(batch, seq, rows):
    if seq >= rows:
        assert seq % rows == 0
        return 1, rows
    nb = min(batch, rows // seq)
    assert batch % nb == 0
    return nb, seq


def _half_tiles(nb, rb):
    if nb > 1:
        return [(slice(h * nb // 2, (h + 1) * nb // 2), slice(None), nb // 2, rb) for h in range(2)]
    return [(slice(None), slice(h * rb // 2, (h + 1) * rb // 2), nb, rb // 2) for h in range(2)]


def _rope_cols(y, cos, sin, first_half):
    outs = []
    for g in range(y.shape[1] // LANES):
        yg = y[:, g * LANES:(g + 1) * LANES]
        rot = jnp.where(first_half, pltpu.roll(yg, LANES - 32, 1), pltpu.roll(yg, 32, 1))
        outs.append(yg * cos + rot * sin)
    return outs[0] if len(outs) == 1 else jnp.concatenate(outs, axis=1)


def _rope_angles(pos0, n):
    half = 32
    inv = ROPE_BASE ** (-np.arange(half, dtype=np.float64) / half)
    pos = (pos0 + np.arange(n)).astype(np.float64)
    return pos[:, None] * inv[None, :]


def _rope_tables(pos0, n):
    ang = _rope_angles(pos0, n)
    cos = np.cos(ang).astype(np.float32)
    sin = np.sin(ang).astype(np.float32)
    return np.concatenate([cos, cos, cos, cos], axis=1), np.concatenate([-sin, sin, -sin, sin], axis=1)


def _rope_tables_t(pos0, n):
    ang = _rope_angles(pos0, n)
    return (np.ascontiguousarray(np.cos(ang).T.astype(np.float32)),
            np.ascontiguousarray(np.sin(ang).T.astype(np.float32)))


def _tile_rows(t, nb):
    if nb == 1:
        return t
    return jnp.broadcast_to(t[None], (nb,) + t.shape).reshape(nb * t.shape[0], t.shape[1])


def _ada_kernel(c_ref, w_ref, b_ref, o_ref):
    c = c_ref[...]
    a = _silu(c).astype(BF16)
    o_ref[...] = _dot(a, w_ref[...].astype(BF16)) + b_ref[...]


def _ada(c, w_ada, b_ada):
    nrow = c.shape[0]
    ncol = w_ada.shape[1]
    tn = ADA_COL_TILE
    return pl.pallas_call(
        _ada_kernel,
        grid=(ncol // tn,),
        in_specs=[_const_spec((nrow, D_MODEL)),
                  pl.BlockSpec((D_MODEL, tn), lambda j: (0, j)),
                  pl.BlockSpec((1, tn), lambda j: (0, j))],
        out_specs=pl.BlockSpec((nrow, tn), lambda j: (0, j)),
        out_shape=jax.ShapeDtypeStruct((nrow, ncol), F32),
        compiler_params=_params(("arbitrary",)),
        name="ada_mod",
    )(c, w_ada, b_ada.reshape(1, ncol))


IN_W = 2 * RET_QK + 2 * RET_WIDTH + Q_LORA + KV_LORA + MLA_ROPE


def _inproj_kernel(x_ref, sc_ref, sh_ref, w_ref, gcq_ref, gckv_ref, cos_ref, sin_ref, *rest, nb, rb, ncast):
    cast_in = rest[:ncast]
    rq_ref, rk_ref, rv_ref, rg_ref, cq_ref, ckv_ref, kr_ref = rest[ncast:ncast + 7]
    cast_out = rest[ncast + 7:]
    for src, dst in zip(cast_in, cast_out):
        dst[...] = src[0].astype(BF16)

    def rmsnorm(y, g):
        ms = jnp.mean(y * y, axis=-1, keepdims=True)
        return y * lax.rsqrt(ms + EPS) * g

    for bsl, rsl, hb_, hr in _half_tiles(nb, rb):
        m = hb_ * hr
        h = x_ref[bsl, rsl, :] * (1.0 + sc_ref[bsl]) + sh_ref[bsl]
        hb = h.astype(BF16).reshape(m, D_MODEL)
        cos = _tile_rows(cos_ref[rsl, :], hb_)
        sin = _tile_rows(sin_ref[rsl, :], hb_)
        lane = lax.broadcasted_iota(jnp.int32, (m, LANES), 1)
        first_half = (lane % 64) < 32

        def proj(lo, hi):
            return _dot(hb, w_ref[:, lo:hi])

        def put(ref, y):
            ref[bsl, rsl, :] = y.reshape(hb_, hr, y.shape[-1])

        o = 0
        put(rq_ref, (_rope_cols(proj(o, o + RET_QK), cos, sin, first_half) * (RET_DK ** -0.5)).astype(BF16))
        o += RET_QK
        put(rk_ref, _rope_cols(proj(o, o + RET_QK), cos, sin, first_half).astype(BF16))
        o += RET_QK
        put(rv_ref, proj(o, o + RET_WIDTH).astype(BF16))
        o += RET_WIDTH
        put(rg_ref, proj(o, o + RET_WIDTH).astype(BF16))
        o += RET_WIDTH
        put(cq_ref, rmsnorm(proj(o, o + Q_LORA), gcq_ref[...]).astype(BF16))
        o += Q_LORA
        put(ckv_ref, rmsnorm(proj(o, o + KV_LORA), gckv_ref[...]))
        o += KV_LORA
        kr = proj(o, o + MLA_ROPE)
        kr = _rope_cols(jnp.concatenate([kr, kr], axis=1), cos, sin, first_half)
        put(kr_ref, kr[:, :MLA_ROPE])


def _inproj(x, sc, sh, w_in_b, g_cq, g_ckv, cos, sin, cast=(), layer=0):
    batch, seq, _ = x.shape
    nb, rb = _row_tiling(batch, seq, ROW_TILE)
    grid = (batch // nb, seq // rb)
    nsteps = grid[0] * grid[1]

    def cast_specs(w):
        rows, share = w.shape[1] // nsteps, 1
        while rows % BF16_SUBLANES or w.shape[1] % rows:
            rows, share = rows * 2, share * 2
        assert nsteps % share == 0
        step = lambda b, i: (b * grid[1] + i) // share
        return (pl.BlockSpec((1, rows, w.shape[2]), lambda b, i: (layer, step(b, i), 0)),
                pl.BlockSpec((rows, w.shape[2]), lambda b, i: (step(b, i), 0)))

    cast_io = [cast_specs(w) for w in cast]
    row = lambda w: pl.BlockSpec((nb, rb, w), lambda b, i: (b, i, 0))
    mod = pl.BlockSpec((nb, 1, D_MODEL), lambda b, i: (b, 0, 0))
    tab = pl.BlockSpec((rb, LANES), lambda b, i: (i, 0))
    sds = lambda w, dt: jax.ShapeDtypeStruct((batch, seq, w), dt)
    return pl.pallas_call(
        functools.partial(_inproj_kernel, nb=nb, rb=rb, ncast=len(cast)),
        grid=grid,
        in_specs=[row(D_MODEL), mod, mod, _const_spec((D_MODEL, IN_W)),
                  _const_spec((1, Q_LORA)), _const_spec((1, KV_LORA)), tab, tab] + [io[0] for io in cast_io],
        out_specs=[row(RET_QK), row(RET_QK), row(RET_WIDTH), row(RET_WIDTH),
                   row(Q_LORA), row(KV_LORA), row(MLA_ROPE)] + [io[1] for io in cast_io],
        out_shape=[sds(RET_QK, BF16), sds(RET_QK, BF16), sds(RET_WIDTH, BF16), sds(RET_WIDTH, BF16),
                   sds(Q_LORA, BF16), sds(KV_LORA, F32), sds(MLA_ROPE, F32)]
                  + [jax.ShapeDtypeStruct(w.shape[1:], BF16) for w in cast],
        compiler_params=_params(("arbitrary", "arbitrary") if cast else ("parallel", "parallel")),
        name="in_proj",
    )(x, sc, sh, w_in_b, g_cq, g_ckv, cos, sin, *cast)


def _ret_tables(c):
    hh = np.arange(RET_HEADS, dtype=np.float64)
    logg = np.log1p(-np.exp2(-5.0 - hh))
    i = np.arange(c, dtype=np.float64)
    diff = i[:, None] - i[None, :]
    dmask = np.where(diff[None] >= 0.0, np.exp(np.maximum(diff, 0.0)[None] * logg[:, None, None]), 0.0)
    qdec = np.exp((i + 1.0)[:, None] * logg[None, :])
    kdec = np.exp((c - 1.0 - i)[:, None] * logg[None, :])
    cdec = np.exp(c * logg)
    qdec = np.repeat(qdec, RET_DK, axis=1)
    kdec = np.repeat(kdec, RET_DK, axis=1)
    r = np.arange(2 * RET_DK)[:, None] // RET_DK
    col = np.arange(2 * RET_DV)[None, :] // RET_DV
    pair = np.arange(RET_HEADS // 2)[:, None, None]
    sdec = np.where(r[None] == col[None], cdec[2 * pair + r[None]], 0.0)
    return tuple(a.astype(np.float32) for a in (dmask, qdec, kdec, sdec))


def _ret_kernel(rq_ref, rk_ref, rv_ref, rg_ref, s0_ref, dmask_ref, qdec_ref, kdec_ref, sdec_ref,
                gret_ref, bret_ref, o_ref, so_ref, s_scr, *, c, nc):
    n = pl.program_id(1)
    npair = RET_HEADS // 2

    @pl.when(n == 0)
    def _():
        s_scr[...] = jnp.zeros(s_scr.shape, F32)
        for p in range(npair):
            s_scr[p, 0:RET_DK, 0:RET_DV] = s0_ref[0, 2 * p]
            s_scr[p, RET_DK:2 * RET_DK, RET_DV:2 * RET_DV] = s0_ref[0, 2 * p + 1]

    for ci in range(nc):
        rows = slice(ci * c, (ci + 1) * c)
        q = rq_ref[0, rows]
        k = rk_ref[0, rows]
        qd = (q.astype(F32) * qdec_ref[...]).astype(BF16)
        kd = (k.astype(F32) * kdec_ref[...]).astype(BF16)
        lane = lax.broadcasted_iota(jnp.int32, (c, LANES), 1)
        lo = lane < RET_DK
        zero = jnp.zeros((c, LANES), BF16)
        for p in range(npair):
            qp = q[:, p * LANES:(p + 1) * LANES]
            kp = k[:, p * LANES:(p + 1) * LANES]
            vp = rv_ref[0, rows, p * 2 * RET_DV:(p + 1) * 2 * RET_DV]
            sc0 = _dot_nt(jnp.where(lo, qp, zero), kp) * dmask_ref[2 * p]
            sc1 = _dot_nt(jnp.where(lo, zero, qp), kp) * dmask_ref[2 * p + 1]
            intra0 = _dot(sc0.astype(BF16), vp[:, :RET_DV])
            intra1 = _dot(sc1.astype(BF16), vp[:, RET_DV:])
            s = s_scr[p]
            inter = _dot(qd[:, p * LANES:(p + 1) * LANES], s.astype(BF16))
            kv = _dot_tn(kd[:, p * LANES:(p + 1) * LANES], vp)
            sdec = sdec_ref[p]
            s_scr[p] = s * sdec + jnp.where(sdec > 0.0, kv, 0.0)
            for hh in range(2):
                hd = 2 * p + hh
                cols = slice(hd * RET_DV, (hd + 1) * RET_DV)
                oh = (intra0 if hh == 0 else intra1) + inter[:, hh * RET_DV:(hh + 1) * RET_DV]
                mu = jnp.mean(oh, axis=-1, keepdims=True)
                d = oh - mu
                var = jnp.mean(d * d, axis=-1, keepdims=True)
                y = d * lax.rsqrt(var + EPS) * gret_ref[:, cols] + bret_ref[:, cols]
                gate = rg_ref[0, rows, cols].astype(F32)
                o_ref[0, rows, cols] = (_silu(gate) * y).astype(BF16)

    @pl.when(n == pl.num_programs(1) - 1)
    def _():
        for p in range(npair):
            so_ref[0, 2 * p] = s_scr[p, 0:RET_DK, 0:RET_DV]
            so_ref[0, 2 * p + 1] = s_scr[p, RET_DK:2 * RET_DK, RET_DV:2 * RET_DV]


def _retention(rq, rk, rv, rg, s0, g_ret, b_ret):
    batch, seq, _ = rq.shape
    c = min(RET_CHUNK, seq)
    nc = min(RET_CHUNKS_PER_STEP, seq // c)
    assert seq % (c * nc) == 0
    dmask, qdec, kdec, sdec = _ret_tables(c)
    row = lambda w: pl.BlockSpec((1, c * nc, w), lambda b, n: (b, n, 0))
    st = pl.BlockSpec((1, RET_HEADS, RET_DK, RET_DV), lambda b, n: (b, 0, 0, 0))
    return pl.pallas_call(
        functools.partial(_ret_kernel, c=c, nc=nc),
        grid=(batch, seq // (c * nc)),
        in_specs=[row(RET_QK), row(RET_QK), row(RET_WIDTH), row(RET_WIDTH), st,
                  _const_spec(dmask.shape), _const_spec(qdec.shape), _const_spec(kdec.shape),
                  _const_spec(sdec.shape), _const_spec((1, RET_WIDTH)), _const_spec((1, RET_WIDTH))],
        out_specs=[row(RET_WIDTH), st],
        out_shape=[jax.ShapeDtypeStruct((batch, seq, RET_WIDTH), BF16),
                   jax.ShapeDtypeStruct((batch, RET_HEADS, RET_DK, RET_DV), F32)],
        scratch_shapes=[pltpu.VMEM((RET_HEADS // 2, 2 * RET_DK, 2 * RET_DV), F32)],
        compiler_params=_params(("parallel", "arbitrary")),
        name="retention",
    )(rq, rk, rv, rg, s0, dmask, qdec, kdec, sdec, g_ret, b_ret)


def _upproj_kernel(cq_ref, ckv_ref, kr_ref, cos_ref, sin_ref, wqt_ref, wk_ref, wvt_ref,
                   qt_ref, k_ref, vt_ref):
    cq = cq_ref[0]
    ckv = ckv_ref[0].astype(BF16)
    krb = kr_ref[0].astype(BF16)
    cos = cos_ref[...]
    sin = sin_ref[...]
    half = MLA_ROPE // 2
    qt = _dot_nt(wqt_ref[...], cq) * (MLA_SCALE * LOG2E)
    kn = _dot(ckv, wk_ref[...]).astype(BF16)
    vt = _dot_nt(wvt_ref[...], ckv).astype(BF16)
    for hd in range(MLA_HEADS):
        r0 = hd * MLA_QK
        x1 = qt[r0 + MLA_NOPE:r0 + MLA_NOPE + half]
        x2 = qt[r0 + MLA_NOPE + half:r0 + MLA_QK]
        qt_ref[0, hd, 0:MLA_NOPE] = qt[r0:r0 + MLA_NOPE].astype(BF16)
        qt_ref[0, hd, MLA_NOPE:MLA_NOPE + half] = (x1 * cos - x2 * sin).astype(BF16)
        qt_ref[0, hd, MLA_NOPE + half:MLA_QK] = (x2 * cos + x1 * sin).astype(BF16)
        k_ref[0, hd, :, 0:MLA_NOPE] = kn[:, hd * MLA_NOPE:(hd + 1) * MLA_NOPE]
        k_ref[0, hd, :, MLA_NOPE:MLA_QK] = krb
        vt_ref[0, hd, 0:MLA_DV] = vt[hd * MLA_DV:(hd + 1) * MLA_DV]
        vt_ref[0, hd, MLA_DV:MLA_DV_AUG] = jnp.ones((MLA_DV_AUG - MLA_DV, vt.shape[1]), BF16)


def _upproj(cq, ckv, kr, cos_t, sin_t, wqt, wk, wvt):
    batch, seq, _ = cq.shape
    rb = min(UP_ROW_TILE, seq)
    assert seq % rb == 0
    row = lambda w: pl.BlockSpec((1, rb, w), lambda b, i: (b, i, 0))
    tab = pl.BlockSpec((MLA_ROPE // 2, rb), lambda b, i: (0, i))
    head_t = lambda w: pl.BlockSpec((1, MLA_HEADS, w, rb), lambda b, i: (b, 0, 0, i))
    return pl.pallas_call(
        _upproj_kernel,
        grid=(batch, seq // rb),
        in_specs=[row(Q_LORA), row(KV_LORA), row(MLA_ROPE), tab, tab,
                  _const_spec(wqt.shape), _const_spec(wk.shape), _const_spec(wvt.shape)],
        out_specs=[head_t(MLA_QK),
                   pl.BlockSpec((1, MLA_HEADS, rb, MLA_QK), lambda b, i: (b, 0, i, 0)),
                   head_t(MLA_DV_AUG)],
        out_shape=[jax.ShapeDtypeStruct((batch, MLA_HEADS, MLA_QK, seq), BF16),
                   jax.ShapeDtypeStruct((batch, MLA_HEADS, seq, MLA_QK), BF16),
                   jax.ShapeDtypeStruct((batch, MLA_HEADS, MLA_DV_AUG, seq), BF16)],
        compiler_params=_params(("parallel", "parallel")),
        name="mla_up_proj",
    )(cq, ckv, kr, cos_t, sin_t, wqt, wk, wvt)


def _flash_kernel(qt_ref, k_ref, vt_ref, o_ref, m_scr, acc_scr, sa_scr, sb_scr, ma_scr, mb_scr, *, tq, tk):
    i = pl.program_id(2)
    ratio = tq // tk
    m_scr[...] = jnp.full(m_scr.shape, -jnp.inf, F32)
    acc_scr[...] = jnp.zeros(acc_scr.shape, F32)

    heads = range(ATTN_HEADS_PER_STEP)

    def scores(j, hd, c0=0):
        start = pl.multiple_of(j * tk, tk)
        return _dot(k_ref[0, hd, pl.ds(start, tk), :], qt_ref[0, hd, :, c0:])

    def update(buf, j, hd, diag=None):
        c0 = 0 if diag is None else diag * tk
        start = pl.multiple_of(j * tk, tk)
        vtb = vt_ref[0, hd, :, pl.ds(start, tk)]
        s = buf[0][hd, :, c0:]
        if diag is None:
            smax = buf[1][hd]
        else:
            kc = (c0 + lax.broadcasted_iota(jnp.int32, (tk, tq - c0), 0)) // CHUNK
            qc = (c0 + lax.broadcasted_iota(jnp.int32, (tk, tq - c0), 1)) // CHUNK
            s = jnp.where(kc <= qc, s, NEG_BIG)
            smax = jnp.max(s, axis=0, keepdims=True)
        m_prev = m_scr[hd, :, c0:]
        m_new = jnp.maximum(m_prev, smax)
        alpha = jnp.exp2(m_prev - m_new)
        p = jnp.exp2(s - m_new)
        acc_scr[hd, :, c0:] = alpha * acc_scr[hd, :, c0:] + _dot(vtb, p.astype(BF16))
        m_scr[hd, :, c0:] = m_new

    def produce(buf, j, hd, diag=None):
        c0 = 0 if diag is None else diag * tk
        s = scores(j, hd, c0)
        buf[0][hd, :, c0:] = s
        if diag is None:
            buf[1][hd] = jnp.max(s, axis=0, keepdims=True)

    def stage(cur, nxt, j, diag=None, diag_next=None):
        for hd in heads:
            produce(nxt, j + 1, hd, diag_next)
            update(cur, j, hd, diag)

    buf_a = (sa_scr, ma_scr)
    buf_b = (sb_scr, mb_scr)
    for hd in heads:
        produce(buf_a, 0, hd)

    def body(jj, carry):
        stage(buf_a, buf_b, 2 * jj)
        stage(buf_b, buf_a, 2 * jj + 1)
        return carry

    nfull = i * ratio
    lax.fori_loop(0, i * (ratio // 2), body, 0)
    bufs = (buf_a, buf_b)
    for d in range(ratio - 1):
        stage(bufs[d % 2], bufs[(d + 1) % 2], nfull + d, d, d + 1)
    for hd in heads:
        update(bufs[(ratio - 1) % 2], nfull + ratio - 1, hd, ratio - 1)

    for hd in heads:
        out_t = acc_scr[hd, 0:MLA_DV] / acc_scr[hd, MLA_DV:MLA_DV + 1]
        o_ref[0, :, hd * MLA_DV:(hd + 1) * MLA_DV] = jnp.transpose(out_t).astype(BF16)


def _flash(qt, k, vt):
    batch, heads, seq, _ = k.shape
    tq = min(ATTN_Q_TILE, seq)
    tk = min(ATTN_K_TILE, seq // 2)
    g = ATTN_HEADS_PER_STEP
    assert seq % tq == 0 and tq % (2 * tk) == 0 and tk % CHUNK == 0 and heads % g == 0
    return pl.pallas_call(
        functools.partial(_flash_kernel, tq=tq, tk=tk),
        grid=(batch, heads // g, seq // tq),
        in_specs=[pl.BlockSpec((1, g, MLA_QK, tq), lambda b, h, i: (b, h, 0, i)),
                  pl.BlockSpec((1, g, seq, MLA_QK), lambda b, h, i: (b, h, 0, 0)),
                  pl.BlockSpec((1, g, MLA_DV_AUG, seq), lambda b, h, i: (b, h, 0, 0))],
        out_specs=pl.BlockSpec((1, tq, g * MLA_DV), lambda b, h, i: (b, i, h)),
        out_shape=jax.ShapeDtypeStruct((batch, seq, MLA_WIDTH), BF16),
        scratch_shapes=[pltpu.VMEM((g, 1, tq), F32),
                        pltpu.VMEM((g, MLA_DV_AUG, tq), F32), pltpu.VMEM((g, tk, tq), F32),
                        pltpu.VMEM((g, tk, tq), F32), pltpu.VMEM((g, 1, tq), F32), pltpu.VMEM((g, 1, tq), F32)],
        compiler_params=_params(("parallel", "parallel", "arbitrary")),
        name="mla_flash",
    )(qt, k, vt)


def _cached_attn_kernel(cq_ref, ckvn_ref, krn_ref, ckvp_ref, krp_ref, cos_ref, sin_ref,
                        wqn_ref, wqr_ref, wkt_ref, wv_ref, o_ref, *, tq, past):
    cq = cq_ref[0]
    lane = lax.broadcasted_iota(jnp.int32, (tq, LANES), 1)
    first_half = (lane % 64) < 32
    qscale = MLA_SCALE * LOG2E
    qn = (_dot(cq, wqn_ref[...]) * qscale).astype(BF16)
    qr = (_rope_cols(_dot(cq, wqr_ref[...]), cos_ref[...], sin_ref[...], first_half) * qscale).astype(BF16)
    qlat = jnp.concatenate(
        [_dot(qn[:, hd * MLA_NOPE:(hd + 1) * MLA_NOPE], wkt_ref[hd]) for hd in range(MLA_HEADS)],
        axis=0).astype(BF16)
    qrr = jnp.concatenate([qr[:, hd * MLA_ROPE:(hd + 1) * MLA_ROPE] for hd in range(MLA_HEADS)], axis=0)
    ckv_p = ckvp_ref[0].astype(BF16)
    kr_p = krp_ref[0].astype(BF16)
    ckv_n = ckvn_ref[0].astype(BF16)
    kr_n = krn_ref[0].astype(BF16)
    s_p = _dot_nt(qlat, ckv_p) + _dot_nt(qrr, kr_p)
    s_n = _dot_nt(qlat, ckv_n) + _dot_nt(qrr, kr_n)
    rows = MLA_HEADS * tq
    qpos = past + lax.broadcasted_iota(jnp.int32, (rows, tq), 0) % tq
    kpos = past + lax.broadcasted_iota(jnp.int32, (rows, tq), 1)
    s_n = jnp.where(kpos // CHUNK <= qpos // CHUNK, s_n, NEG_BIG)
    m = jnp.maximum(jnp.max(s_p, axis=1, keepdims=True), jnp.max(s_n, axis=1, keepdims=True))
    p_p = jnp.exp2(s_p - m)
    p_n = jnp.exp2(s_n - m)
    l = jnp.sum(p_p, axis=1, keepdims=True) + jnp.sum(p_n, axis=1, keepdims=True)
    olat = (_dot(p_p.astype(BF16), ckv_p) + _dot(p_n.astype(BF16), ckv_n)) / l
    olat = olat.astype(BF16)
    for hd in range(MLA_HEADS):
        o_ref[0, :, hd * MLA_DV:(hd + 1) * MLA_DV] = _dot(olat[hd * tq:(hd + 1) * tq], wv_ref[hd]).astype(BF16)


def _cached_attn(cq, ckv_new, kr_new, ckv_past, kr_past, cos, sin, wqn, wqr, wkt, wvh):
    batch, tq, _ = cq.shape
    past = ckv_past.shape[1]
    assert past % CHUNK == 0
    one = lambda n, w: pl.BlockSpec((1, n, w), lambda b: (b, 0, 0))
    return pl.pallas_call(
        functools.partial(_cached_attn_kernel, tq=tq, past=past),
        grid=(batch,),
        in_specs=[one(tq, Q_LORA), one(tq, KV_LORA), one(tq, MLA_ROPE), one(past, KV_LORA), one(past, MLA_ROPE),
                  _const_spec(cos.shape), _const_spec(sin.shape), _const_spec(wqn.shape), _const_spec(wqr.shape),
                  _const_spec(wkt.shape), _const_spec(wvh.shape)],
        out_specs=one(tq, MLA_WIDTH),
        out_shape=jax.ShapeDtypeStruct((batch, tq, MLA_WIDTH), BF16),
        compiler_params=_params(("parallel",)),
        name="mla_cached",
    )(cq, ckv_new, kr_new, ckv_past, kr_past, cos, sin, wqn, wqr, wkt, wvh)


def _layernorm(y, g, b):
    mu = jnp.mean(y, axis=-1, keepdims=True)
    d = y - mu
    var = jnp.mean(d * d, axis=-1, keepdims=True)
    return d * lax.rsqrt(var + EPS) * g + b


def _outproj_kernel(oret_ref, omla_ref, x_ref, g1_ref, w_ref, lng_ref, lnb_ref, x1_ref, *, nb, rb):
    for bsl, rsl, hb, hr in _half_tiles(nb, rb):
        a = oret_ref[bsl, rsl, :].reshape(hb * hr, RET_WIDTH)
        b = omla_ref[bsl, rsl, :].reshape(hb * hr, MLA_WIDTH)
        mix = _dot(a, w_ref[0:RET_WIDTH, :]) + _dot(b, w_ref[RET_WIDTH:RET_WIDTH + MLA_WIDTH, :])
        y = ALPHA * x_ref[bsl, rsl, :] + g1_ref[bsl] * mix.reshape(hb, hr, D_MODEL)
        x1_ref[bsl, rsl, :] = _layernorm(y, lng_ref[...], lnb_ref[...])


def _outproj(o_ret, o_mla, x, g1, w_out_b, ln_g, ln_b):
    batch, seq, _ = x.shape
    nb, rb = _row_tiling(batch, seq, ROW_TILE)
    row = lambda w: pl.BlockSpec((nb, rb, w), lambda b, i: (b, i, 0))
    mod = pl.BlockSpec((nb, 1, D_MODEL), lambda b, i: (b, 0, 0))
    return pl.pallas_call(
        functools.partial(_outproj_kernel, nb=nb, rb=rb),
        grid=(batch // nb, seq // rb),
        in_specs=[row(RET_WIDTH), row(MLA_WIDTH), row(D_MODEL), mod, _const_spec(w_out_b.shape),
                  _const_spec((1, 1, D_MODEL)), _const_spec((1, 1, D_MODEL))],
        out_specs=row(D_MODEL),
        out_shape=jax.ShapeDtypeStruct((batch, seq, D_MODEL), F32),
        compiler_params=_params(("parallel", "parallel")),
        name="out_proj_ln1",
    )(o_ret, o_mla, x, g1, w_out_b, ln_g, ln_b)


def _ffn_kernel(x1_ref, sc_ref, sh_ref, g2_ref, wg_ref, wu_ref, wd_ref, lng_ref, lnb_ref, o_ref, h_scr,
                *, nb, rb):
    m = nb * rb
    j = pl.program_id(2)

    @pl.when(j == 0)
    def _():
        x1 = x1_ref[...]
        h_scr[...] = (x1 * (1.0 + sc_ref[...]) + sh_ref[...]).astype(BF16).reshape(m, D_MODEL)
        o_ref[...] = ALPHA * x1

    h = h_scr[...]
    act = (_silu(_dot(h, wg_ref[...])) * _dot(h, wu_ref[...])).astype(BF16)
    for n in range(D_MODEL // FFN_COL_TILE):
        cols = slice(n * FFN_COL_TILE, (n + 1) * FFN_COL_TILE)
        part = _dot(act, wd_ref[:, cols]).reshape(nb, rb, FFN_COL_TILE)
        o_ref[:, :, cols] += part * g2_ref[:, :, cols]

    @pl.when(j == pl.num_programs(2) - 1)
    def _():
        o_ref[...] = _layernorm(o_ref[...], lng_ref[...], lnb_ref[...])


def _ffn(x1, sc, sh, g2, wg, wu, wd, ln_g, ln_b):
    batch, seq, _ = x1.shape
    nb, rb = _row_tiling(batch, seq, FFN_ROW_TILE)
    tf = FFN_COL_TILE
    assert D_FF % tf == 0
    row = pl.BlockSpec((nb, rb, D_MODEL), lambda b, i, j: (b, i, 0))
    mod = pl.BlockSpec((nb, 1, D_MODEL), lambda b, i, j: (b, 0, 0))
    return pl.pallas_call(
        functools.partial(_ffn_kernel, nb=nb, rb=rb),
        grid=(batch // nb, seq // rb, D_FF // tf),
        in_specs=[row, mod, mod, mod,
                  pl.BlockSpec((D_MODEL, tf), lambda b, i, j: (0, j)),
                  pl.BlockSpec((D_MODEL, tf), lambda b, i, j: (0, j)),
                  pl.BlockSpec((tf, D_MODEL), lambda b, i, j: (j, 0)),
                  _const_spec((1, 1, D_MODEL)), _const_spec((1, 1, D_MODEL))],
        out_specs=row,
        out_shape=jax.ShapeDtypeStruct((batch, seq, D_MODEL), F32),
        scratch_shapes=[pltpu.VMEM((nb * rb, D_MODEL), BF16)],
        compiler_params=_params(("parallel", "parallel", "arbitrary"), FFN_VMEM_LIMIT),
        name="swiglu_ln2",
    )(x1, sc, sh, g2, wg, wu, wd, ln_g, ln_b)


def _layer(x, mods, pos0, ckv_past, kr_past, ret_s0, wts):
    sh1, sc1, g1, sh2, sc2, g2 = mods
    seq = x.shape[1]
    cos, sin = _rope_tables(pos0, seq)
    layer, late = wts.pop("late_f32", (0, {}))
    outs = _inproj(x, sc1, sh1, wts["w_in"], wts["g_cq"], wts["g_ckv"], cos, sin, tuple(late.values()), layer)
    rq, rk, rv, rg, cq, ckv, kr = outs[:7]
    wts.update(zip(late.keys(), outs[7:]))
    o_ret, ret_s1 = _retention(rq, rk, rv, rg, ret_s0, wts["g_ret"], wts["b_ret"])
    if ckv_past is None:
        cos_t, sin_t = _rope_tables_t(pos0, seq)
        qt, k, vt = _upproj(cq, ckv, kr, cos_t, sin_t, wts["wqt"], wts["wk"], wts["wvt"])
        o_mla = _flash(qt, k, vt)
    else:
        o_mla = _cached_attn(cq, ckv, kr, ckv_past, kr_past, cos, sin,
                             wts["wqn"], wts["wqr"], wts["wkt"], wts["wvh"])
    x1 = _outproj(o_ret, o_mla, x, g1, wts["w_out"], wts["ln1_g"], wts["ln1_b"])
    y = _ffn(x1, sc2, sh2, g2, wts["w_gate"], wts["w_up"], wts["w_down"], wts["ln2_g"], wts["ln2_b"])
    return y, ckv, kr, ret_s1


def _prep_weights(l, w_in, g_cq, g_ckv, w_uq, w_uk, w_uv, g_ret, b_ret, w_out, ln1_g, ln1_b,
                  w_gate, w_up, w_down, ln2_g, ln2_b):
    return {
        "w_in": w_in[l].astype(BF16),
        "g_cq": g_cq[l].reshape(1, Q_LORA),
        "g_ckv": g_ckv[l].reshape(1, KV_LORA),
        "wqn": w_uq[l][:, :, :MLA_NOPE].reshape(Q_LORA, MLA_HEADS * MLA_NOPE).astype(BF16),
        "wqr": w_uq[l][:, :, MLA_NOPE:].reshape(Q_LORA, MLA_HEADS * MLA_ROPE).astype(BF16),
        "wk": w_uk[l].reshape(KV_LORA, MLA_HEADS * MLA_NOPE).astype(BF16),
        "wqt": jnp.transpose(w_uq[l].reshape(Q_LORA, MLA_HEADS * MLA_QK)).astype(BF16),
        "wvt": jnp.transpose(w_uv[l].reshape(KV_LORA, MLA_HEADS * MLA_DV)).astype(BF16),
        "wkt": jnp.transpose(w_uk[l], (1, 2, 0)).astype(BF16),
        "wvh": jnp.transpose(w_uv[l], (1, 0, 2)).astype(BF16),
        "g_ret": g_ret[l].reshape(1, RET_WIDTH),
        "b_ret": b_ret[l].reshape(1, RET_WIDTH),
        "ln1_g": ln1_g[l].reshape(1, 1, D_MODEL),
        "ln1_b": ln1_b[l].reshape(1, 1, D_MODEL),
        "late_f32": (l, {"w_out": w_out, "w_gate": w_gate, "w_up": w_up, "w_down": w_down}),
        "ln2_g": ln2_g[l].reshape(1, 1, D_MODEL),
        "ln2_b": ln2_b[l].reshape(1, 1, D_MODEL),
    }


def kernel(x_prompt, x_sample, c_prompt, c_sample, cache_mla_ckv, cache_mla_krope, state_ret, w_ada, b_ada, w_in, g_cq, g_ckv, w_uq, w_uk, w_uv, g_ret, b_ret, w_out, ln1_g, ln1_b, w_gate, w_up, w_down, ln2_g, ln2_b):
    bp, tp, _ = x_prompt.shape
    bs, ts, _ = x_sample.shape
    past = cache_mla_ckv.shape[2]
    c_all = jnp.concatenate([c_prompt, c_sample], axis=0)
    yp, ys = x_prompt, x_sample
    outs = [[] for _ in range(6)]
    for l in range(w_ada.shape[0]):
        wts = _prep_weights(l, w_in, g_cq, g_ckv, w_uq, w_uk, w_uv, g_ret, b_ret, w_out, ln1_g, ln1_b,
                            w_gate, w_up, w_down, ln2_g, ln2_b)
        mod = _ada(c_all, w_ada[l], b_ada[l]).reshape(bp + bs, 6, 1, D_MODEL)
        mods_p = tuple(mod[:bp, i] for i in range(6))
        mods_s = tuple(mod[bp:, i] for i in range(6))
        s0 = jnp.zeros((bp, RET_HEADS, RET_DK, RET_DV), F32)
        yp, ckv_p, kr_p, rs_p = _layer(yp, mods_p, 0, None, None, s0, wts)
        ys, ckv_s, kr_s, rs_s = _layer(ys, mods_s, past, cache_mla_ckv[l], cache_mla_krope[l],
                                       state_ret[l], wts)
        for lst, val in zip(outs, (ckv_p, kr_p, rs_p, ckv_s, kr_s, rs_s)):
            lst.append(val)
    stacked = [v[0][None] if len(v) == 1 else jnp.stack(v, axis=0) for v in outs]
    return (yp, ys, *stacked)
```

```python
import functools
import math

import jax
import jax.numpy as jnp
import numpy as np
from jax import lax
from jax.experimental import pallas as pl
from jax.experimental.pallas import tpu as pltpu

F32 = jnp.float32
BF16 = jnp.bfloat16

D_MODEL = 2048
CHUNK = 64
RET_HEADS = 8
RET_DK = 64
RET_DV = 128
RET_QK = RET_HEADS * RET_DK
RET_WIDTH = RET_HEADS * RET_DV
MLA_HEADS = 8
MLA_NOPE = 128
MLA_ROPE = 64
MLA_DV = 128
MLA_DV_AUG = MLA_DV + 16
MLA_QK = MLA_NOPE + MLA_ROPE
MLA_WIDTH = MLA_HEADS * MLA_DV
Q_LORA = 512
KV_LORA = 512
MLA_SCALE = MLA_QK ** -0.5
D_FF = 5632
DEPTH = 1
ALPHA = (2 * DEPTH) ** 0.25
ROPE_BASE = 10000.0
EPS = 1e-5
LOG2E = math.log2(math.e)
NEG_BIG = -1e30

LANES = 128
BF16_SUBLANES = 16
ROW_TILE = 512
FFN_ROW_TILE = 1024
FFN_COL_TILE = 512
RET_CHUNK = 256
RET_CHUNKS_PER_STEP = 4
UP_ROW_TILE = 1024
ATTN_Q_TILE = 1024
ATTN_K_TILE = 512
ATTN_HEADS_PER_STEP = 2
ADA_COL_TILE = 1024
VMEM_LIMIT = 56 * 1024 * 1024
FFN_VMEM_LIMIT = 60 * 1024 * 1024


def _params(sem, vmem=VMEM_LIMIT):
    return pltpu.CompilerParams(dimension_semantics=sem, vmem_limit_bytes=vmem)


def _const_spec(shape):
    zeros = (0,) * len(shape)
    return pl.BlockSpec(shape, lambda *_: zeros, pipeline_mode=pl.Buffered(1))


def _dot(a, b):
    return jnp.dot(a, b, preferred_element_type=F32)


def _dot_nt(a, b):
    return lax.dot_general(a, b, (((1,), (1,)), ((), ())), preferred_element_type=F32)


def _dot_tn(a, b):
    return lax.dot_general(a, b, (((0,), (0,)), ((), ())), preferred_element_type=F32)


def _silu(x):
    return x * jax.nn.sigmoid(x)


def _row_tiling(batch, seq, rows):
    if seq >= rows:
        assert seq % rows == 0
        return 1, rows
    nb = min(batch, rows // seq)
    assert batch % nb == 0
    return nb, seq


def _half_tiles(nb, rb):
    if nb > 1:
        return [(slice(h * nb // 2, (h + 1) * nb // 2), slice(None), nb // 2, rb) for h in range(2)]
    return [(slice(None), slice(h * rb // 2, (h + 1) * rb // 2), nb, rb // 2) for h in range(2)]


def _rope_cols(y, cos, sin, first_half):
    outs = []
    for g in range(y.shape[1] // LANES):
        yg = y[:, g * LANES:(g + 1) * LANES]
        rot = jnp.where(first_half, pltpu.roll(yg, LANES - 32, 1), pltpu.roll(yg, 32, 1))
        outs.append(yg * cos + rot * sin)
    return outs[0] if len(outs) == 1 else jnp.concatenate(outs, axis=1)


def _rope_angles(pos0, n):
    half = 32
    inv = ROPE_BASE ** (-np.arange(half, dtype=np.float64) / half)
    pos = (pos0 + np.arange(n)).astype(np.float64)
    return pos[:, None] * inv[None, :]


def _rope_tables(pos0, n):
    ang = _rope_angles(pos0, n)
    cos = np.cos(ang).astype(np.float32)
    sin = np.sin(ang).astype(np.float32)
    return np.concatenate([cos, cos, cos, cos], axis=1), np.concatenate([-sin, sin, -sin, sin], axis=1)


def _rope_tables_t(pos0, n):
    ang = _rope_angles(pos0, n)
    return (np.ascontiguousarray(np.cos(ang).T.astype(np.float32)),
            np.ascontiguousarray(np.sin(ang).T.astype(np.float32)))


def _tile_rows(t, nb):
    if nb == 1:
        return t
    return jnp.broadcast_to(t[None], (nb,) + t.shape).reshape(nb * t.shape[0], t.shape[1])


def _ada_kernel(c_ref, w_ref, b_ref, o_ref):
    c = c_ref[...]
    a = _silu(c).astype(BF16)
    o_ref[...] = _dot(a, w_ref[...].astype(BF16)) + b_ref[...]


def _ada(c, w_ada, b_ada):
    nrow = c.shape[0]
    ncol = w_ada.shape[1]
    tn = ADA_COL_TILE
    return pl.pallas_call(
        _ada_kernel,
        grid=(ncol // tn,),
        in_specs=[_const_spec((nrow, D_MODEL)),
                  pl.BlockSpec((D_MODEL, tn), lambda j: (0, j)),
                  pl.BlockSpec((1, tn), lambda j: (0, j))],
        out_specs=pl.BlockSpec((nrow, tn), lambda j: (0, j)),
        out_shape=jax.ShapeDtypeStruct((nrow, ncol), F32),
        compiler_params=_params(("arbitrary",)),
        name="ada_mod",
    )(c, w_ada, b_ada.reshape(1, ncol))


IN_W = 2 * RET_QK + 2 * RET_WIDTH + Q_LORA + KV_LORA + MLA_ROPE


def _inproj_kernel(x_ref, sc_ref, sh_ref, w_ref, gcq_ref, gckv_ref, cos_ref, sin_ref, *rest, nb, rb, ncast):
    cast_in = rest[:ncast]
    rq_ref, rk_ref, rv_ref, rg_ref, cq_ref, ckv_ref, kr_ref = rest[ncast:ncast + 7]
    cast_out = rest[ncast + 7:]
    for src, dst in zip(cast_in, cast_out):
        dst[...] = src[0].astype(BF16)

    def rmsnorm(y, g):
        ms = jnp.mean(y * y, axis=-1, keepdims=True)
        return y * lax.rsqrt(ms + EPS) * g

    for bsl, rsl, hb_, hr in _half_tiles(nb, rb):
        m = hb_ * hr
        h = x_ref[bsl, rsl, :] * (1.0 + sc_ref[bsl]) + sh_ref[bsl]
        hb = h.astype(BF16).reshape(m, D_MODEL)
        cos = _tile_rows(cos_ref[rsl, :], hb_)
        sin = _tile_rows(sin_ref[rsl, :], hb_)
        lane = lax.broadcasted_iota(jnp.int32, (m, LANES), 1)
        first_half = (lane % 64) < 32

        def proj(lo, hi):
            return _dot(hb, w_ref[:, lo:hi])

        def put(ref, y):
            ref[bsl, rsl, :] = y.reshape(hb_, hr, y.shape[-1])

        o = 0
        put(rq_ref, (_rope_cols(proj(o, o + RET_QK), cos, sin, first_half) * (RET_DK ** -0.5)).astype(BF16))
        o += RET_QK
        put(rk_ref, _rope_cols(proj(o, o + RET_QK), cos, sin, first_half).astype(BF16))
        o += RET_QK
        put(rv_ref, proj(o, o + RET_WIDTH).astype(BF16))
        o += RET_WIDTH
        put(rg_ref, proj(o, o + RET_WIDTH).astype(BF16))
        o += RET_WIDTH
        put(cq_ref, rmsnorm(proj(o, o + Q_LORA), gcq_ref[...]).astype(BF16))
        o += Q_LORA
        put(ckv_ref, rmsnorm(proj(o, o + KV_LORA), gckv_ref[...]))
        o += KV_LORA
        kr = proj(o, o + MLA_ROPE)
        kr = _rope_cols(jnp.concatenate([kr, kr], axis=1), cos, sin, first_half)
        put(kr_ref, kr[:, :MLA_ROPE])


def _inproj(x, sc, sh, w_in_b, g_cq, g_ckv, cos, sin, cast=(), layer=0):
    batch, seq, _ = x.shape
    nb, rb = _row_tiling(batch, seq, ROW_TILE)
    grid = (batch // nb, seq // rb)
    nsteps = grid[0] * grid[1]

    def cast_specs(w):
        rows, share = w.shape[1] // nsteps, 1
        while rows % BF16_SUBLANES or w.shape[1] % rows:
            rows, share = rows * 2, share * 2
        assert nsteps % share == 0
        step = lambda b, i: (b * grid[1] + i) // share
        return (pl.BlockSpec((1, rows, w.shape[2]), lambda b, i: (layer, step(b, i), 0)),
                pl.BlockSpec((rows, w.shape[2]), lambda b, i: (step(b, i), 0)))

    cast_io = [cast_specs(w) for w in cast]
    row = lambda w: pl.BlockSpec((nb, rb, w), lambda b, i: (b, i, 0))
    mod = pl.BlockSpec((nb, 1, D_MODEL), lambda b, i: (b, 0, 0))
    tab = pl.BlockSpec((rb, LANES), lambda b, i: (i, 0))
    sds = lambda w, dt: jax.ShapeDtypeStruct((batch, seq, w), dt)
    return pl.pallas_call(
        functools.partial(_inproj_kernel, nb=nb, rb=rb, ncast=len(cast)),
        grid=grid,
        in_specs=[row(D_MODEL), mod, mod, _const_spec((D_MODEL, IN_W)),
                  _const_spec((1, Q_LORA)), _const_spec((1, KV_LORA)), tab, tab] + [io[0] for io in cast_io],
        out_specs=[row(RET_QK), row(RET_QK), row(RET_WIDTH), row(RET_WIDTH),
                   row(Q_LORA), row(KV_LORA), row(MLA_ROPE)] + [io[1] for io in cast_io],
        out_shape=[sds(RET_QK, BF16), sds(RET_QK, BF16), sds(RET_WIDTH, BF16), sds(RET_WIDTH, BF16),
                   sds(Q_LORA, BF16), sds(KV_LORA, F32), sds(MLA_ROPE, F32)]
                  + [jax.ShapeDtypeStruct(w.shape[1:], BF16) for w in cast],
        compiler_params=_params(("arbitrary", "arbitrary") if cast else ("parallel", "parallel")),
        name="in_proj",
    )(x, sc, sh, w_in_b, g_cq, g_ckv, cos, sin, *cast)


def _ret_tables(c):
    hh = np.arange(RET_HEADS, dtype=np.float64)
    logg = np.log1p(-np.exp2(-5.0 - hh))
    i = np.arange(c, dtype=np.float64)
    diff = i[:, None] - i[None, :]
    dmask = np.where(diff[None] >= 0.0, np.exp(np.maximum(diff, 0.0)[None] * logg[:, None, None]), 0.0)
    qdec = np.exp((i + 1.0)[:, None] * logg[None, :])
    kdec = np.exp((c - 1.0 - i)[:, None] * logg[None, :])
    cdec = np.exp(c * logg)
    qdec = np.repeat(qdec, RET_DK, axis=1)
    kdec = np.repeat(kdec, RET_DK, axis=1)
    r = np.arange(2 * RET_DK)[:, None] // RET_DK
    col = np.arange(2 * RET_DV)[None, :] // RET_DV
    pair = np.arange(RET_HEADS // 2)[:, None, None]
    sdec = np.where(r[None] == col[None], cdec[2 * pair + r[None]], 0.0)
    return tuple(a.astype(np.float32) for a in (dmask, qdec, kdec, sdec))


def _ret_kernel(rq_ref, rk_ref, rv_ref, rg_ref, s0_ref, dmask_ref, qdec_ref, kdec_ref, sdec_ref,
                gret_ref, bret_ref, o_ref, so_ref, s_scr, *, c, nc):
    n = pl.program_id(1)
    npair = RET_HEADS // 2

    @pl.when(n == 0)
    def _():
        s_scr[...] = jnp.zeros(s_scr.shape, F32)
        for p in range(npair):
            s_scr[p, 0:RET_DK, 0:RET_DV] = s0_ref[0, 2 * p]
            s_scr[p, RET_DK:2 * RET_DK, RET_DV:2 * RET_DV] = s0_ref[0, 2 * p + 1]

    for ci in range(nc):
        rows = slice(ci * c, (ci + 1) * c)
        q = rq_ref[0, rows]
        k = rk_ref[0, rows]
        qd = (q.astype(F32) * qdec_ref[...]).astype(BF16)
        kd = (k.astype(F32) * kdec_ref[...]).astype(BF16)
        lane = lax.broadcasted_iota(jnp.int32, (c, LANES), 1)
        lo = lane < RET_DK
        zero = jnp.zeros((c, LANES), BF16)
        for p in range(npair):
            qp = q[:, p * LANES:(p + 1) * LANES]
            kp = k[:, p * LANES:(p + 1) * LANES]
            vp = rv_ref[0, rows, p * 2 * RET_DV:(p + 1) * 2 * RET_DV]
            sc0 = _dot_nt(jnp.where(lo, qp, zero), kp) * dmask_ref[2 * p]
            sc1 = _dot_nt(jnp.where(lo, zero, qp), kp) * dmask_ref[2 * p + 1]
            intra0 = _dot(sc0.astype(BF16), vp[:, :RET_DV])
            intra1 = _dot(sc1.astype(BF16), vp[:, RET_DV:])
            s = s_scr[p]
            inter = _dot(qd[:, p * LANES:(p + 1) * LANES], s.astype(BF16))
            kv = _dot_tn(kd[:, p * LANES:(p + 1) * LANES], vp)
            sdec = sdec_ref[p]
            s_scr[p] = s * sdec + jnp.where(sdec > 0.0, kv, 0.0)
            for hh in range(2):
                hd = 2 * p + hh
                cols = slice(hd * RET_DV, (hd + 1) * RET_DV)
                oh = (intra0 if hh == 0 else intra1) + inter[:, hh * RET_DV:(hh + 1) * RET_DV]
                mu = jnp.mean(oh, axis=-1, keepdims=True)
                d = oh - mu
                var = jnp.mean(d * d, axis=-1, keepdims=True)
                y = d * lax.rsqrt(var + EPS) * gret_ref[:, cols] + bret_ref[:, cols]
                gate = rg_ref[0, rows, cols].astype(F32)
                o_ref[0, rows, cols] = (_silu(gate) * y).astype(BF16)

    @pl.when(n == pl.num_programs(1) - 1)
    def _():
        for p in range(npair):
            so_ref[0, 2 * p] = s_scr[p, 0:RET_DK, 0:RET_DV]
            so_ref[0, 2 * p + 1] = s_scr[p, RET_DK:2 * RET_DK, RET_DV:2 * RET_DV]


def _retention(rq, rk, rv, rg, s0, g_ret, b_ret):
    batch, seq, _ = rq.shape
    c = min(RET_CHUNK, seq)
    nc = min(RET_CHUNKS_PER_STEP, seq // c)
    assert seq % (c * nc) == 0
    dmask, qdec, kdec, sdec = _ret_tables(c)
    row = lambda w: pl.BlockSpec((1, c * nc, w), lambda b, n: (b, n, 0))
    st = pl.BlockSpec((1, RET_HEADS, RET_DK, RET_DV), lambda b, n: (b, 0, 0, 0))
    return pl.pallas_call(
        functools.partial(_ret_kernel, c=c, nc=nc),
        grid=(batch, seq // (c * nc)),
        in_specs=[row(RET_QK), row(RET_QK), row(RET_WIDTH), row(RET_WIDTH), st,
                  _const_spec(dmask.shape), _const_spec(qdec.shape), _const_spec(kdec.shape),
                  _const_spec(sdec.shape), _const_spec((1, RET_WIDTH)), _const_spec((1, RET_WIDTH))],
        out_specs=[row(RET_WIDTH), st],
        out_shape=[jax.ShapeDtypeStruct((batch, seq, RET_WIDTH), BF16),
                   jax.ShapeDtypeStruct((batch, RET_HEADS, RET_DK, RET_DV), F32)],
        scratch_shapes=[pltpu.VMEM((RET_HEADS // 2, 2 * RET_DK, 2 * RET_DV), F32)],
        compiler_params=_params(("parallel", "arbitrary")),
        name="retention",
    )(rq, rk, rv, rg, s0, dmask, qdec, kdec, sdec, g_ret, b_ret)


def _upproj_kernel(cq_ref, ckv_ref, kr_ref, cos_ref, sin_ref, wqt_ref, wk_ref, wvt_ref,
                   qt_ref, k_ref, vt_ref):
    cq = cq_ref[0]
    ckv = ckv_ref[0].astype(BF16)
    krb = kr_ref[0].astype(BF16)
    cos = cos_ref[...]
    sin = sin_ref[...]
    half = MLA_ROPE // 2
    qt = _dot_nt(wqt_ref[...], cq) * (MLA_SCALE * LOG2E)
    kn = _dot(ckv, wk_ref[...]).astype(BF16)
    vt = _dot_nt(wvt_ref[...], ckv).astype(BF16)
    for hd in range(MLA_HEADS):
        r0 = hd * MLA_QK
        x1 = qt[r0 + MLA_NOPE:r0 + MLA_NOPE + half]
        x2 = qt[r0 + MLA_NOPE + half:r0 + MLA_QK]
        qt_ref[0, hd, 0:MLA_NOPE] = qt[r0:r0 + MLA_NOPE].astype(BF16)
        qt_ref[0, hd, MLA_NOPE:MLA_NOPE + half] = (x1 * cos - x2 * sin).astype(BF16)
        qt_ref[0, hd, MLA_NOPE + half:MLA_QK] = (x2 * cos + x1 * sin).astype(BF16)
        k_ref[0, hd, :, 0:MLA_NOPE] = kn[:, hd * MLA_NOPE:(hd + 1) * MLA_NOPE]
        k_ref[0, hd, :, MLA_NOPE:MLA_QK] = krb
        vt_ref[0, hd, 0:MLA_DV] = vt[hd * MLA_DV:(hd + 1) * MLA_DV]
        vt_ref[0, hd, MLA_DV:MLA_DV_AUG] = jnp.ones((MLA_DV_AUG - MLA_DV, vt.shape[1]), BF16)


def _upproj(cq, ckv, kr, cos_t, sin_t, wqt, wk, wvt):
    batch, seq, _ = cq.shape
    rb = min(UP_ROW_TILE, seq)
    assert seq % rb == 0
    row = lambda w: pl.BlockSpec((1, rb, w), lambda b, i: (b, i, 0))
    tab = pl.BlockSpec((MLA_ROPE // 2, rb), lambda b, i: (0, i))
    head_t = lambda w: pl.BlockSpec((1, MLA_HEADS, w, rb), lambda b, i: (b, 0, 0, i))
    return pl.pallas_call(
        _upproj_kernel,
        grid=(batch, seq // rb),
        in_specs=[row(Q_LORA), row(KV_LORA), row(MLA_ROPE), tab, tab,
                  _const_spec(wqt.shape), _const_spec(wk.shape), _const_spec(wvt.shape)],
        out_specs=[head_t(MLA_QK),
                   pl.BlockSpec((1, MLA_HEADS, rb, MLA_QK), lambda b, i: (b, 0, i, 0)),
                   head_t(MLA_DV_AUG)],
        out_shape=[jax.ShapeDtypeStruct((batch, MLA_HEADS, MLA_QK, seq), BF16),
                   jax.ShapeDtypeStruct((batch, MLA_HEADS, seq, MLA_QK), BF16),
                   jax.ShapeDtypeStruct((batch, MLA_HEADS, MLA_DV_AUG, seq), BF16)],
        compiler_params=_params(("parallel", "parallel")),
        name="mla_up_proj",
    )(cq, ckv, kr, cos_t, sin_t, wqt, wk, wvt)


def _flash_kernel(*refs, tq, tk, nq):
    def qtile(i, carry):
        _flash_tile(i, *refs, tq=tq, tk=tk)
        return carry

    lax.fori_loop(0, nq, qtile, 0)


def _flash_tile(i, qt_ref, k_ref, vt_ref, o_ref, m_scr, acc_scr, sa_scr, sb_scr, ma_scr, mb_scr, *, tq, tk):
    q0 = pl.multiple_of(i * tq, tq)
    ratio = tq // tk
    m_scr[...] = jnp.full(m_scr.shape, -jnp.inf, F32)
    acc_scr[...] = jnp.zeros(acc_scr.shape, F32)

    heads = range(ATTN_HEADS_PER_STEP)

    def scores(j, hd, c0=0):
        start = pl.multiple_of(j * tk, tk)
        qcols = pl.ds(pl.multiple_of(q0 + c0, tk), tq - c0)
        return _dot(k_ref[0, hd, pl.ds(start, tk), :], qt_ref[0, hd, :, qcols])

    def update(buf, j, hd, diag=None):
        c0 = 0 if diag is None else diag * tk
        start = pl.multiple_of(j * tk, tk)
        vtb = vt_ref[0, hd, :, pl.ds(start, tk)]
        s = buf[0][hd, :, c0:]
        if diag is None:
            smax = buf[1][hd]
        else:
            kc = (c0 + lax.broadcasted_iota(jnp.int32, (tk, tq - c0), 0)) // CHUNK
            qc = (c0 + lax.broadcasted_iota(jnp.int32, (tk, tq - c0), 1)) // CHUNK
            s = jnp.where(kc <= qc, s, NEG_BIG)
            smax = jnp.max(s, axis=0, keepdims=True)
        m_prev = m_scr[hd, :, c0:]
        m_new = jnp.maximum(m_prev, smax)
        alpha = jnp.exp2(m_prev - m_new)
        p = jnp.exp2(s - m_new)
        acc_scr[hd, :, c0:] = alpha * acc_scr[hd, :, c0:] + _dot(vtb, p.astype(BF16))
        m_scr[hd, :, c0:] = m_new

    def produce(buf, j, hd, diag=None):
        c0 = 0 if diag is None else diag * tk
        s = scores(j, hd, c0)
        buf[0][hd, :, c0:] = s
        if diag is None:
            buf[1][hd] = jnp.max(s, axis=0, keepdims=True)

    def stage(cur, nxt, j, diag=None, diag_next=None):
        for hd in heads:
            produce(nxt, j + 1, hd, diag_next)
            update(cur, j, hd, diag)

    buf_a = (sa_scr, ma_scr)
    buf_b = (sb_scr, mb_scr)
    for hd in heads:
        produce(buf_a, 0, hd)

    def body(jj, carry):
        stage(buf_a, buf_b, 2 * jj)
        stage(buf_b, buf_a, 2 * jj + 1)
        return carry

    nfull = i * ratio
    lax.fori_loop(0, i * (ratio // 2), body, 0)
    bufs = (buf_a, buf_b)
    for d in range(ratio - 1):
        stage(bufs[d % 2], bufs[(d + 1) % 2], nfull + d, d, d + 1)
    for hd in heads:
        update(bufs[(ratio - 1) % 2], nfull + ratio - 1, hd, ratio - 1)

    for hd in heads:
        out_t = acc_scr[hd, 0:MLA_DV] / acc_scr[hd, MLA_DV:MLA_DV + 1]
        o_ref[0, pl.ds(q0, tq), hd * MLA_DV:(hd + 1) * MLA_DV] = jnp.transpose(out_t).astype(BF16)


def _flash(qt, k, vt):
    batch, heads, seq, _ = k.shape
    tq = min(ATTN_Q_TILE, seq)
    tk = min(ATTN_K_TILE, seq // 2)
    g = ATTN_HEADS_PER_STEP
    assert seq % tq == 0 and tq % (2 * tk) == 0 and tk % CHUNK == 0 and heads % g == 0
    return pl.pallas_call(
        functools.partial(_flash_kernel, tq=tq, tk=tk, nq=seq // tq),
        grid=(batch, heads // g),
        in_specs=[pl.BlockSpec((1, g, MLA_QK, seq), lambda b, h: (b, h, 0, 0)),
                  pl.BlockSpec((1, g, seq, MLA_QK), lambda b, h: (b, h, 0, 0)),
                  pl.BlockSpec((1, g, MLA_DV_AUG, seq), lambda b, h: (b, h, 0, 0))],
        out_specs=pl.BlockSpec((1, seq, g * MLA_DV), lambda b, h: (b, 0, h)),
        out_shape=jax.ShapeDtypeStruct((batch, seq, MLA_WIDTH), BF16),
        scratch_shapes=[pltpu.VMEM((g, 1, tq), F32),
                        pltpu.VMEM((g, MLA_DV_AUG, tq), F32), pltpu.VMEM((g, tk, tq), F32),
                        pltpu.VMEM((g, tk, tq), F32), pltpu.VMEM((g, 1, tq), F32), pltpu.VMEM((g, 1, tq), F32)],
        compiler_params=_params(("parallel", "parallel")),
        name="mla_flash",
    )(qt, k, vt)


def _cached_attn_kernel(cq_ref, ckvn_ref, krn_ref, ckvp_ref, krp_ref, cos_ref, sin_ref,
                        wqn_ref, wqr_ref, wkt_ref, wv_ref, o_ref, *, tq, past):
    cq = cq_ref[0]
    lane = lax.broadcasted_iota(jnp.int32, (tq, LANES), 1)
    first_half = (lane % 64) < 32
    qscale = MLA_SCALE * LOG2E
    qn = (_dot(cq, wqn_ref[...]) * qscale).astype(BF16)
    qr = (_rope_cols(_dot(cq, wqr_ref[...]), cos_ref[...], sin_ref[...], first_half) * qscale).astype(BF16)
    qlat = jnp.concatenate(
        [_dot(qn[:, hd * MLA_NOPE:(hd + 1) * MLA_NOPE], wkt_ref[hd]) for hd in range(MLA_HEADS)],
        axis=0).astype(BF16)
    qrr = jnp.concatenate([qr[:, hd * MLA_ROPE:(hd + 1) * MLA_ROPE] for hd in range(MLA_HEADS)], axis=0)
    ckv_p = ckvp_ref[0].astype(BF16)
    kr_p = krp_ref[0].astype(BF16)
    ckv_n = ckvn_ref[0].astype(BF16)
    kr_n = krn_ref[0].astype(BF16)
    s_p = _dot_nt(qlat, ckv_p) + _dot_nt(qrr, kr_p)
    s_n = _dot_nt(qlat, ckv_n) + _dot_nt(qrr, kr_n)
    rows = MLA_HEADS * tq
    qpos = past + lax.broadcasted_iota(jnp.int32, (rows, tq), 0) % tq
    kpos = past + lax.broadcasted_iota(jnp.int32, (rows, tq), 1)
    s_n = jnp.where(kpos // CHUNK <= qpos // CHUNK, s_n, NEG_BIG)
    m = jnp.maximum(jnp.max(s_p, axis=1, keepdims=True), jnp.max(s_n, axis=1, keepdims=True))
    p_p = jnp.exp2(s_p - m)
    p_n = jnp.exp2(s_n - m)
    l = jnp.sum(p_p, axis=1, keepdims=True) + jnp.sum(p_n, axis=1, keepdims=True)
    olat = (_dot(p_p.astype(BF16), ckv_p) + _dot(p_n.astype(BF16), ckv_n)) / l
    olat = olat.astype(BF16)
    for hd in range(MLA_HEADS):
        o_ref[0, :, hd * MLA_DV:(hd + 1) * MLA_DV] = _dot(olat[hd * tq:(hd + 1) * tq], wv_ref[hd]).astype(BF16)


def _cached_attn(cq, ckv_new, kr_new, ckv_past, kr_past, cos, sin, wqn, wqr, wkt, wvh):
    batch, tq, _ = cq.shape
    past = ckv_past.shape[1]
    assert past % CHUNK == 0
    one = lambda n, w: pl.BlockSpec((1, n, w), lambda b: (b, 0, 0))
    return pl.pallas_call(
        functools.partial(_cached_attn_kernel, tq=tq, past=past),
        grid=(batch,),
        in_specs=[one(tq, Q_LORA), one(tq, KV_LORA), one(tq, MLA_ROPE), one(past, KV_LORA), one(past, MLA_ROPE),
                  _const_spec(cos.shape), _const_spec(sin.shape), _const_spec(wqn.shape), _const_spec(wqr.shape),
                  _const_spec(wkt.shape), _const_spec(wvh.shape)],
        out_specs=one(tq, MLA_WIDTH),
        out_shape=jax.ShapeDtypeStruct((batch, tq, MLA_WIDTH), BF16),
        compiler_params=_params(("parallel",)),
        name="mla_cached",
    )(cq, ckv_new, kr_new, ckv_past, kr_past, cos, sin, wqn, wqr, wkt, wvh)


def _layernorm(y, g, b):
    mu = jnp.mean(y, axis=-1, keepdims=True)
    d = y - mu
    var = jnp.mean(d * d, axis=-1, keepdims=True)
    return d * lax.rsqrt(var + EPS) * g + b


def _outproj_kernel(oret_ref, omla_ref, x_ref, g1_ref, w_ref, lng_ref, lnb_ref, x1_ref, *, nb, rb):
    for bsl, rsl, hb, hr in _half_tiles(nb, rb):
        a = oret_ref[bsl, rsl, :].reshape(hb * hr, RET_WIDTH)
        b = omla_ref[bsl, rsl, :].reshape(hb * hr, MLA_WIDTH)
        mix = _dot(a, w_ref[0:RET_WIDTH, :]) + _dot(b, w_ref[RET_WIDTH:RET_WIDTH + MLA_WIDTH, :])
        y = ALPHA * x_ref[bsl, rsl, :] + g1_ref[bsl] * mix.reshape(hb, hr, D_MODEL)
        x1_ref[bsl, rsl, :] = _layernorm(y, lng_ref[...], lnb_ref[...])


def _outproj(o_ret, o_mla, x, g1, w_out_b, ln_g, ln_b):
    batch, seq, _ = x.shape
    nb, rb = _row_tiling(batch, seq, ROW_TILE)
    row = lambda w: pl.BlockSpec((nb, rb, w), lambda b, i: (b, i, 0))
    mod = pl.BlockSpec((nb, 1, D_MODEL), lambda b, i: (b, 0, 0))
    return pl.pallas_call(
        functools.partial(_outproj_kernel, nb=nb, rb=rb),
        grid=(batch // nb, seq // rb),
        in_specs=[row(RET_WIDTH), row(MLA_WIDTH), row(D_MODEL), mod, _const_spec(w_out_b.shape),
                  _const_spec((1, 1, D_MODEL)), _const_spec((1, 1, D_MODEL))],
        out_specs=row(D_MODEL),
        out_shape=jax.ShapeDtypeStruct((batch, seq, D_MODEL), F32),
        compiler_params=_params(("parallel", "parallel")),
        name="out_proj_ln1",
    )(o_ret, o_mla, x, g1, w_out_b, ln_g, ln_b)


def _ffn_kernel(x1_ref, sc_ref, sh_ref, g2_ref, wg_ref, wu_ref, wd_ref, lng_ref, lnb_ref, o_ref, h_scr,
                *, nb, rb):
    m = nb * rb
    j = pl.program_id(2)

    @pl.when(j == 0)
    def _():
        x1 = x1_ref[...]
        h_scr[...] = (x1 * (1.0 + sc_ref[...]) + sh_ref[...]).astype(BF16).reshape(m, D_MODEL)
        o_ref[...] = ALPHA * x1

    h = h_scr[...]
    act = (_silu(_dot(h, wg_ref[...])) * _dot(h, wu_ref[...])).astype(BF16)
    for n in range(D_MODEL // FFN_COL_TILE):
        cols = slice(n * FFN_COL_TILE, (n + 1) * FFN_COL_TILE)
        part = _dot(act, wd_ref[:, cols]).reshape(nb, rb, FFN_COL_TILE)
        o_ref[:, :, cols] += part * g2_ref[:, :, cols]

    @pl.when(j == pl.num_programs(2) - 1)
    def _():
        o_ref[...] = _layernorm(o_ref[...], lng_ref[...], lnb_ref[...])


def _ffn(x1, sc, sh, g2, wg, wu, wd, ln_g, ln_b):
    batch, seq, _ = x1.shape
    nb, rb = _row_tiling(batch, seq, FFN_ROW_TILE)
    tf = FFN_COL_TILE
    assert D_FF % tf == 0
    row = pl.BlockSpec((nb, rb, D_MODEL), lambda b, i, j: (b, i, 0))
    mod = pl.BlockSpec((nb, 1, D_MODEL), lambda b, i, j: (b, 0, 0))
    return pl.pallas_call(
        functools.partial(_ffn_kernel, nb=nb, rb=rb),
        grid=(batch // nb, seq // rb, D_FF // tf),
        in_specs=[row, mod, mod, mod,
                  pl.BlockSpec((D_MODEL, tf), lambda b, i, j: (0, j)),
                  pl.BlockSpec((D_MODEL, tf), lambda b, i, j: (0, j)),
                  pl.BlockSpec((tf, D_MODEL), lambda b, i, j: (j, 0)),
                  _const_spec((1, 1, D_MODEL)), _const_spec((1, 1, D_MODEL))],
        out_specs=row,
        out_shape=jax.ShapeDtypeStruct((batch, seq, D_MODEL), F32),
        scratch_shapes=[pltpu.VMEM((nb * rb, D_MODEL), BF16)],
        compiler_params=_params(("parallel", "parallel", "arbitrary"), FFN_VMEM_LIMIT),
        name="swiglu_ln2",
    )(x1, sc, sh, g2, wg, wu, wd, ln_g, ln_b)


def _layer(x, mods, pos0, ckv_past, kr_past, ret_s0, wts):
    sh1, sc1, g1, sh2, sc2, g2 = mods
    seq = x.shape[1]
    cos, sin = _rope_tables(pos0, seq)
    layer, late = wts.pop("late_f32", (0, {}))
    outs = _inproj(x, sc1, sh1, wts["w_in"], wts["g_cq"], wts["g_ckv"], cos, sin, tuple(late.values()), layer)
    rq, rk, rv, rg, cq, ckv, kr = outs[:7]
    wts.update(zip(late.keys(), outs[7:]))
    o_ret, ret_s1 = _retention(rq, rk, rv, rg, ret_s0, wts["g_ret"], wts["b_ret"])
    if ckv_past is None:
        cos_t, sin_t = _rope_tables_t(pos0, seq)
        qt, k, vt = _upproj(cq, ckv, kr, cos_t, sin_t, wts["wqt"], wts["wk"], wts["wvt"])
        o_mla = _flash(qt, k, vt)
    else:
        o_mla = _cached_attn(cq, ckv, kr, ckv_past, kr_past, cos, sin,
                             wts["wqn"], wts["wqr"], wts["wkt"], wts["wvh"])
    x1 = _outproj(o_ret, o_mla, x, g1, wts["w_out"], wts["ln1_g"], wts["ln1_b"])
    y = _ffn(x1, sc2, sh2, g2, wts["w_gate"], wts["w_up"], wts["w_down"], wts["ln2_g"], wts["ln2_b"])
    return y, ckv, kr, ret_s1


def _prep_weights(l, w_in, g_cq, g_ckv, w_uq, w_uk, w_uv, g_ret, b_ret, w_out, ln1_g, ln1_b,
                  w_gate, w_up, w_down, ln2_g, ln2_b):
    return {
        "w_in": w_in[l].astype(BF16),
        "g_cq": g_cq[l].reshape(1, Q_LORA),
        "g_ckv": g_ckv[l].reshape(1, KV_LORA),
        "wqn": w_uq[l][:, :, :MLA_NOPE].reshape(Q_LORA, MLA_HEADS * MLA_NOPE).astype(BF16),
        "wqr": w_uq[l][:, :, MLA_NOPE:].reshape(Q_LORA, MLA_HEADS * MLA_ROPE).astype(BF16),
        "wk": w_uk[l].reshape(KV_LORA, MLA_HEADS * MLA_NOPE).astype(BF16),
        "wqt": jnp.transpose(w_uq[l].reshape(Q_LORA, MLA_HEADS * MLA_QK)).astype(BF16),
        "wvt": jnp.transpose(w_uv[l].reshape(KV_LORA, MLA_HEADS * MLA_DV)).astype(BF16),
        "wkt": jnp.transpose(w_uk[l], (1, 2, 0)).astype(BF16),
        "wvh": jnp.transpose(w_uv[l], (1, 0, 2)).astype(BF16),
        "g_ret": g_ret[l].reshape(1, RET_WIDTH),
        "b_ret": b_ret[l].reshape(1, RET_WIDTH),
        "ln1_g": ln1_g[l].reshape(1, 1, D_MODEL),
        "ln1_b": ln1_b[l].reshape(1, 1, D_MODEL),
        "late_f32": (l, {"w_out": w_out, "w_gate": w_gate, "w_up": w_up, "w_down": w_down}),
        "ln2_g": ln2_g[l].reshape(1, 1, D_MODEL),
        "ln2_b": ln2_b[l].reshape(1, 1, D_MODEL),
    }


def kernel(x_prompt, x_sample, c_prompt, c_sample, cache_mla_ckv, cache_mla_krope, state_ret, w_ada, b_ada, w_in, g_cq, g_ckv, w_uq, w_uk, w_uv, g_ret, b_ret, w_out, ln1_g, ln1_b, w_gate, w_up, w_down, ln2_g, ln2_b):
    bp, tp, _ = x_prompt.shape
    bs, ts, _ = x_sample.shape
    past = cache_mla_ckv.shape[2]
    c_all = jnp.concatenate([c_prompt, c_sample], axis=0)
    yp, ys = x_prompt, x_sample
    outs = [[] for _ in range(6)]
    for l in range(w_ada.shape[0]):
        wts = _prep_weights(l, w_in, g_cq, g_ckv, w_uq, w_uk, w_uv, g_ret, b_ret, w_out, ln1_g, ln1_b,
                            w_gate, w_up, w_down, ln2_g, ln2_b)
        mod = _ada(c_all, w_ada[l], b_ada[l]).reshape(bp + bs, 6, 1, D_MODEL)
        mods_p = tuple(mod[:bp, i] for i in range(6))
        mods_s = tuple(mod[bp:, i] for i in range(6))
        s0 = jnp.zeros((bp, RET_HEADS, RET_DK, RET_DV), F32)
        yp, ckv_p, kr_p, rs_p = _layer(yp, mods_p, 0, None, None, s0, wts)
        ys, ckv_s, kr_s, rs_s = _layer(ys, mods_s, past, cache_mla_ckv[l], cache_mla_krope[l],
                                       state_ret[l], wts)
        for lst, val in zip(outs, (ckv_p, kr_p, rs_p, ckv_s, kr_s, rs_s)):
            lst.append(val)
    stacked = [v[0][None] if len(v) == 1 else jnp.stack(v, axis=0) for v in outs]
    return (yp, ys, *stacked)
```
